```python
import math, functools
import jax, jax.numpy as jnp
from jax import lax
import numpy as np

D_MODEL = 1024
BATCH = 1
SEQ = 16384
DEPTH = 2
DEC_BATCH = 128
DEC_SEQ = 1
PAST_LEN = 16384
PAGE_SIZE = 128

MLA_HEADS = 8
QK_NOPE = 64
QK_ROPE = 32
V_DIM = 64
Q_LORA = 384
KV_LORA = 256
MLA_WIDTH = MLA_HEADS * V_DIM
ATTN_SCALE = 1.0 / math.sqrt(QK_NOPE + QK_ROPE)
ROPE_THETA = 10000.0
Q_BLOCK = 128
S5_GROUPS = 16
S5_GROUP_CH = 16
S5_STATE = 64
S5_WIDTH = S5_GROUPS * S5_GROUP_CH
GM_HEADS = 4
GM_HEAD_DIM = 64
GM_WIDTH = GM_HEADS * GM_HEAD_DIM
CHUNK = 128
D_FF = 4 * D_MODEL
PLE_DIM = 256
EPS = 1e-6
NEG_INIT = -1e30
SPLIT_SIZES = (Q_LORA, KV_LORA, QK_ROPE, S5_WIDTH, GM_WIDTH, GM_WIDTH)
SPLIT_OFFSETS = (Q_LORA, Q_LORA + KV_LORA, Q_LORA + KV_LORA + QK_ROPE,
                 Q_LORA + KV_LORA + QK_ROPE + S5_WIDTH,
                 Q_LORA + KV_LORA + QK_ROPE + S5_WIDTH + GM_WIDTH)
W_IN_COLS = Q_LORA + KV_LORA + QK_ROPE + S5_WIDTH + 2 * GM_WIDTH

kernel_name = 'hymba_mla_s5_chunkgmlp_decode_step'


def rms_norm(x, g):
    xf = x.astype(jnp.float32)
    y = xf * lax.rsqrt(jnp.mean(xf * xf, axis=-1, keepdims=True) + EPS)
    return (y * g.astype(jnp.float32)).astype(x.dtype)


def rope_cos_sin(positions):
    inv = ROPE_THETA ** (-jnp.arange(0, QK_ROPE, 2, dtype=jnp.float32) / QK_ROPE)
    ang = positions.astype(jnp.float32)[:, None] * inv[None, :]
    return jnp.cos(ang), jnp.sin(ang)


def apply_rope(x, cos, sin):
    xf = x.astype(jnp.float32)
    x1, x2 = jnp.split(xf, 2, axis=-1)
    return jnp.concatenate([x1 * cos - x2 * sin, x2 * cos + x1 * sin], axis=-1).astype(x.dtype)


def mla_project(c_q, c_kv, k_rope, positions, g_q, w_uq, g_kv):
    B, T, _ = c_q.shape
    q = (rms_norm(c_q, g_q) @ w_uq).reshape(B, T, MLA_HEADS, QK_NOPE + QK_ROPE)
    q_nope, q_rope = q[..., :QK_NOPE], q[..., QK_NOPE:]
    cos, sin = rope_cos_sin(positions)
    q_rope = apply_rope(q_rope, cos[:, None, :], sin[:, None, :])
    kr = apply_rope(k_rope, cos, sin)
    ckv = rms_norm(c_kv, g_kv)
    return q_nope, q_rope, ckv, kr


def mla_prompt_attention(q_nope, q_rope, ckv, kr, w_uk, w_uv):
    B, T = q_nope.shape[0], q_nope.shape[1]
    k_nope = jnp.einsum('btc,chd->bthd', ckv, w_uk)
    v = jnp.einsum('btc,chd->bthd', ckv, w_uv)
    nb = T // Q_BLOCK
    qn_b = q_nope.reshape(B, nb, Q_BLOCK, MLA_HEADS, QK_NOPE).transpose(1, 0, 2, 3, 4)
    qr_b = q_rope.reshape(B, nb, Q_BLOCK, MLA_HEADS, QK_ROPE).transpose(1, 0, 2, 3, 4)
    starts = jnp.arange(nb) * Q_BLOCK
    kpos = jnp.arange(T)

    def block(args):
        qn, qr, start = args
        s = (jnp.einsum('bqhd,bkhd->bhqk', qn, k_nope, preferred_element_type=jnp.float32)
             + jnp.einsum('bqhr,bkr->bhqk', qr, kr, preferred_element_type=jnp.float32)) * ATTN_SCALE
        qpos = start + jnp.arange(Q_BLOCK)
        s = jnp.where(kpos[None, :] <= qpos[:, None], s, -jnp.inf)
        p = jax.nn.softmax(s, axis=-1).astype(v.dtype)
        return jnp.einsum('bhqk,bkhd->bqhd', p, v)

    out = lax.map(block, (qn_b, qr_b, starts))
    return out.transpose(1, 0, 2, 3, 4).reshape(B, T, MLA_WIDTH)


def mla_sample_attention(q_nope, q_rope, ckv_new, kr_new, w_uk, w_uv,
                         cache_ckv_l, cache_kr_l, page_table):
    DB, Q = q_nope.shape[0], q_nope.shape[1]
    q_lat = jnp.einsum('bqhd,chd->bqhc', q_nope, w_uk)

    def step(carry, pages):
        m, l, acc = carry
        kc = cache_ckv_l[pages]
        kp = cache_kr_l[pages]
        s = (jnp.einsum('bqhc,bkc->bhqk', q_lat, kc, preferred_element_type=jnp.float32)
             + jnp.einsum('bqhr,bkr->bhqk', q_rope, kp, preferred_element_type=jnp.float32)) * ATTN_SCALE
        m_new = jnp.maximum(m, jnp.max(s, axis=-1))
        corr = jnp.exp(m - m_new)
        p = jnp.exp(s - m_new[..., None])
        l = l * corr + jnp.sum(p, axis=-1)
        acc = acc * corr[..., None] + jnp.einsum('bhqk,bkc->bhqc', p, kc.astype(jnp.float32))
        return (m_new, l, acc), None

    init = (jnp.full((DB, MLA_HEADS, Q), NEG_INIT, jnp.float32),
            jnp.zeros((DB, MLA_HEADS, Q), jnp.float32),
            jnp.zeros((DB, MLA_HEADS, Q, KV_LORA), jnp.float32))
    (m, l, acc), _ = lax.scan(step, init, page_table.T)
    s = (jnp.einsum('bqhc,bkc->bhqk', q_lat, ckv_new, preferred_element_type=jnp.float32)
         + jnp.einsum('bqhr,bkr->bhqk', q_rope, kr_new, preferred_element_type=jnp.float32)) * ATTN_SCALE
    causal = jnp.tril(jnp.ones((Q, Q), dtype=bool))
    s = jnp.where(causal, s, -jnp.inf)
    m_new = jnp.maximum(m, jnp.max(s, axis=-1))
    corr = jnp.exp(m - m_new)
    p = jnp.exp(s - m_new[..., None])
    l = l * corr + jnp.sum(p, axis=-1)
    acc = acc * corr[..., None] + jnp.einsum('bhqk,bkc->bhqc', p, ckv_new.astype(jnp.float32))
    o_lat = (acc / l[..., None]).astype(w_uv.dtype)
    out = jnp.einsum('bhqc,chd->bqhd', o_lat, w_uv)
    return out.reshape(DB, Q, MLA_WIDTH).astype(q_nope.dtype)


def s5_mixer(u, h0_re, h0_im, a_re, a_im, log_dt, b_re, b_im, c_re, c_im, d_skip, glu_w, glu_b):
    f32 = jnp.float32
    B, T, _ = u.shape
    uf = u.astype(f32)
    ug = uf.reshape(B, T, S5_GROUPS, S5_GROUP_CH)
    a_re = a_re.astype(f32)
    a_im = a_im.astype(f32)
    dt = jnp.exp(log_dt.astype(f32))[:, None]
    mag = jnp.exp(a_re * dt)
    ab_re = mag * jnp.cos(a_im * dt)
    ab_im = mag * jnp.sin(a_im * dt)
    den = a_re * a_re + a_im * a_im
    n_re = ab_re - 1.0
    n_im = ab_im
    k_re = (n_re * a_re + n_im * a_im) / den
    k_im = (n_im * a_re - n_re * a_im) / den
    b_re = b_re.astype(f32)
    b_im = b_im.astype(f32)
    bb_re = k_re[..., None] * b_re - k_im[..., None] * b_im
    bb_im = k_re[..., None] * b_im + k_im[..., None] * b_re
    bu_re = jnp.einsum('btgh,gph->btgp', ug, bb_re)
    bu_im = jnp.einsum('btgh,gph->btgp', ug, bb_im)
    if h0_re is not None:
        h0r = h0_re.astype(f32)
        h0i = h0_im.astype(f32)
        bu_re = bu_re.at[:, 0].add(ab_re * h0r - ab_im * h0i)
        bu_im = bu_im.at[:, 0].add(ab_re * h0i + ab_im * h0r)
    ar = jnp.broadcast_to(ab_re, bu_re.shape)
    ai = jnp.broadcast_to(ab_im, bu_im.shape)

    def combine(e1, e2):
        a1r, a1i, b1r, b1i = e1
        a2r, a2i, b2r, b2i = e2
        return (a2r * a1r - a2i * a1i, a2r * a1i + a2i * a1r,
                a2r * b1r - a2i * b1i + b2r, a2r * b1i + a2i * b1r + b2i)

    _, _, h_re, h_im = lax.associative_scan(combine, (ar, ai, bu_re, bu_im), axis=1)
    y = (jnp.einsum('btgp,gkp->btgk', h_re, c_re.astype(f32))
         - jnp.einsum('btgp,gkp->btgk', h_im, c_im.astype(f32)))
    y = y.reshape(B, T, S5_WIDTH) + d_skip.astype(f32) * uf
    g = jax.nn.gelu(y)
    z = g @ glu_w.astype(f32) + glu_b.astype(f32)
    z1, z2 = jnp.split(z, 2, axis=-1)
    out = (z1 * jax.nn.sigmoid(z2)).astype(u.dtype)
    return out, h_re[:, -1], h_im[:, -1]


def chunk_gating(u, v, g_v, w_s, b_s):
    B, T, _ = u.shape
    v = rms_norm(v, g_v)
    pad = (-T) % CHUNK
    vp = jnp.pad(v, ((0, 0), (0, pad), (0, 0)))
    nc = (T + pad) // CHUNK
    vc = vp.reshape(B, nc, CHUNK, GM_HEADS, GM_HEAD_DIM)
    w = w_s * jnp.tril(jnp.ones((CHUNK, CHUNK), dtype=w_s.dtype))
    s = jnp.einsum('hij,bnjhd->bnihd', w, vc) + b_s.T[None, None, :, :, None]
    s = s.reshape(B, nc * CHUNK, GM_WIDTH)[:, :T]
    return u * s, v


def decoder_layer(h, ple, positions, attend, s5_h0_re, s5_h0_im,
                  g_mix, w_in, g_q, w_uq, g_kv, w_uk, w_uv,
                  s5_a_re, s5_a_im, s5_log_dt, s5_b_re, s5_b_im, s5_c_re, s5_c_im,
                  s5_d, s5_glu_w, s5_glu_b, g_gm_v, gm_ws, gm_bs,
                  w_out, g_ffn, w_up, w_down, g_ple, w_ple, w_pg):
    a = rms_norm(h, g_mix)
    z = a @ w_in
    c_q, c_kv, k_rope, u_s5, u_gm, v_gm = jnp.split(z, list(SPLIT_OFFSETS), axis=-1)
    q_nope, q_rope, ckv, kr = mla_project(c_q, c_kv, k_rope, positions, g_q, w_uq, g_kv)
    o_mla = attend(q_nope, q_rope, ckv, kr, w_uk, w_uv)
    o_s5, s_re, s_im = s5_mixer(u_s5, s5_h0_re, s5_h0_im, s5_a_re, s5_a_im, s5_log_dt,
                                s5_b_re, s5_b_im, s5_c_re, s5_c_im, s5_d, s5_glu_w, s5_glu_b)
    o_gm, v_n = chunk_gating(u_gm, v_gm, g_gm_v, gm_ws, gm_bs)
    h = h + jnp.concatenate([o_mla, o_s5, o_gm], axis=-1) @ w_out
    f = jax.nn.relu(rms_norm(h, g_ffn) @ w_up)
    h = h + jnp.square(f) @ w_down
    h = h + (ple @ w_ple) * jax.nn.sigmoid(rms_norm(h, g_ple) @ w_pg)
    return h, ckv, kr, s_re, s_im, v_n


def setup_inputs(seed: int = 0) -> dict:
    key = jax.random.key(seed)
    ks = iter(jax.random.split(key, 48))
    f32 = jnp.float32

    def nrm(shape, scale=1.0):
        return jax.random.normal(next(ks), shape, f32) * scale

    def gain(shape):
        return 1.0 + 0.02 * jax.random.normal(next(ks), shape, f32)

    n_pages = PAST_LEN // PAGE_SIZE
    used = DEC_BATCH * n_pages
    n_pool = used + max(1, used // 4)
    page_table = jax.random.permutation(next(ks), n_pool)[:used].reshape(DEC_BATCH, n_pages).astype(jnp.int32)

    inputs = {}
    inputs['x_prompt'] = nrm((BATCH, SEQ, D_MODEL))
    inputs['x_sample'] = nrm((DEC_BATCH, DEC_SEQ, D_MODEL))
    inputs['cache_ckv'] = nrm((DEPTH, n_pool, PAGE_SIZE, KV_LORA))
    inputs['cache_krope'] = nrm((DEPTH, n_pool, PAGE_SIZE, QK_ROPE))
    inputs['state_s5_re'] = nrm((DEPTH, DEC_BATCH, S5_GROUPS, S5_STATE), 0.1)
    inputs['state_s5_im'] = nrm((DEPTH, DEC_BATCH, S5_GROUPS, S5_STATE), 0.1)
    inputs['page_table'] = page_table
    inputs['p_prompt'] = nrm((DEPTH, BATCH, SEQ, PLE_DIM))
    inputs['p_sample'] = nrm((DEPTH, DEC_BATCH, DEC_SEQ, PLE_DIM))
    inputs['g_mix'] = gain((DEPTH, D_MODEL))
    inputs['w_in'] = nrm((DEPTH, D_MODEL, W_IN_COLS), D_MODEL ** -0.5)
    inputs['g_q'] = gain((DEPTH, Q_LORA))
    inputs['w_uq'] = nrm((DEPTH, Q_LORA, MLA_HEADS * (QK_NOPE + QK_ROPE)), Q_LORA ** -0.5)
    inputs['g_kv'] = gain((DEPTH, KV_LORA))
    inputs['w_uk'] = nrm((DEPTH, KV_LORA, MLA_HEADS, QK_NOPE), KV_LORA ** -0.5)
    inputs['w_uv'] = nrm((DEPTH, KV_LORA, MLA_HEADS, V_DIM), KV_LORA ** -0.5)
    inputs['s5_a_re'] = -0.5 + 0.01 * nrm((DEPTH, S5_GROUPS, S5_STATE))
    inputs['s5_a_im'] = jnp.broadcast_to(math.pi * jnp.arange(S5_STATE, dtype=f32),
                                         (DEPTH, S5_GROUPS, S5_STATE)) + 0.01 * nrm((DEPTH, S5_GROUPS, S5_STATE))
    inputs['s5_log_dt'] = jax.random.uniform(next(ks), (DEPTH, S5_GROUPS), f32,
                                             math.log(1e-3), math.log(1e-1))
    inputs['s5_b_re'] = nrm((DEPTH, S5_GROUPS, S5_STATE, S5_GROUP_CH), (2 * S5_GROUP_CH) ** -0.5)
    inputs['s5_b_im'] = nrm((DEPTH, S5_GROUPS, S5_STATE, S5_GROUP_CH), (2 * S5_GROUP_CH) ** -0.5)
    inputs['s5_c_re'] = nrm((DEPTH, S5_GROUPS, S5_GROUP_CH, S5_STATE), S5_STATE ** -0.5)
    inputs['s5_c_im'] = nrm((DEPTH, S5_GROUPS, S5_GROUP_CH, S5_STATE), S5_STATE ** -0.5)
    inputs['s5_d'] = nrm((DEPTH, S5_WIDTH))
    inputs['s5_glu_w'] = nrm((DEPTH, S5_WIDTH, 2 * S5_WIDTH), S5_WIDTH ** -0.5)
    inputs['s5_glu_b'] = nrm((DEPTH, 2 * S5_WIDTH), 0.02)
    inputs['g_gm_v'] = gain((DEPTH, GM_WIDTH))
    inputs['gm_ws'] = nrm((DEPTH, GM_HEADS, CHUNK, CHUNK), 0.5 * CHUNK ** -0.5)
    inputs['gm_bs'] = 1.0 + 0.1 * nrm((DEPTH, GM_HEADS, CHUNK))
    inputs['w_out'] = nrm((DEPTH, D_MODEL, D_MODEL), D_MODEL ** -0.5)
    inputs['g_ffn'] = gain((DEPTH, D_MODEL))
    inputs['w_up'] = nrm((DEPTH, D_MODEL, D_FF), D_MODEL ** -0.5)
    inputs['w_down'] = nrm((DEPTH, D_FF, D_MODEL), D_FF ** -0.5)
    inputs['g_ple'] = gain((DEPTH, D_MODEL))
    inputs['w_ple'] = nrm((DEPTH, PLE_DIM, D_MODEL), PLE_DIM ** -0.5)
    inputs['w_pg'] = nrm((DEPTH, D_MODEL, D_MODEL), D_MODEL ** -0.5)
    inputs['g_final'] = gain((D_MODEL,))
    return inputs


def reference(x_prompt, x_sample, cache_ckv, cache_krope, state_s5_re, state_s5_im, page_table,
              p_prompt, p_sample, g_mix, w_in, g_q, w_uq, g_kv, w_uk, w_uv,
              s5_a_re, s5_a_im, s5_log_dt, s5_b_re, s5_b_im, s5_c_re, s5_c_im, s5_d,
              s5_glu_w, s5_glu_b, g_gm_v, gm_ws, gm_bs, w_out, g_ffn, w_up, w_down,
              g_ple, w_ple, w_pg, g_final):
    pos_p = jnp.arange(x_prompt.shape[1])
    pos_s = PAST_LEN + jnp.arange(x_sample.shape[1])
    hp, hs = x_prompt, x_sample
    ckv_p_l, kr_p_l, ckv_s_l, kr_s_l = [], [], [], []
    sre_p_l, sim_p_l, sre_s_l, sim_s_l, v_s_l = [], [], [], [], []
    for l in range(DEPTH):
        lw = (g_mix[l], w_in[l], g_q[l], w_uq[l], g_kv[l], w_uk[l], w_uv[l],
              s5_a_re[l], s5_a_im[l], s5_log_dt[l], s5_b_re[l], s5_b_im[l], s5_c_re[l], s5_c_im[l],
              s5_d[l], s5_glu_w[l], s5_glu_b[l], g_gm_v[l], gm_ws[l], gm_bs[l],
              w_out[l], g_ffn[l], w_up[l], w_down[l], g_ple[l], w_ple[l], w_pg[l])
        hp, ckv_p, kr_p, sre_p, sim_p, _ = decoder_layer(
            hp, p_prompt[l], pos_p, mla_prompt_attention, None, None, *lw)
        attend_s = functools.partial(mla_sample_attention, cache_ckv_l=cache_ckv[l],
                                     cache_kr_l=cache_krope[l], page_table=page_table)
        hs, ckv_s, kr_s, sre_s, sim_s, v_s = decoder_layer(
            hs, p_sample[l], pos_s, attend_s, state_s5_re[l], state_s5_im[l], *lw)
        ckv_p_l.append(ckv_p)
        kr_p_l.append(kr_p)
        ckv_s_l.append(ckv_s)
        kr_s_l.append(kr_s)
        sre_p_l.append(sre_p)
        sim_p_l.append(sim_p)
        sre_s_l.append(sre_s)
        sim_s_l.append(sim_s)
        v_s_l.append(v_s)
    y_prompt = rms_norm(hp, g_final)
    y_sample = rms_norm(hs, g_final)
    return (y_prompt, y_sample,
            jnp.stack(ckv_p_l), jnp.stack(kr_p_l), jnp.stack(ckv_s_l), jnp.stack(kr_s_l),
            jnp.stack(sre_p_l), jnp.stack(sim_p_l), jnp.stack(sre_s_l), jnp.stack(sim_s_l),
            jnp.stack(v_s_l))
```

```python
import functools
import math

import numpy as np
import jax
import jax.numpy as jnp
from jax import lax
from jax.experimental import pallas as pl
from jax.experimental.pallas import tpu as pltpu

F32 = jnp.float32
BF16 = jnp.bfloat16

EPS = 1e-6
ROPE_THETA = 10000.0
NEG_INIT = -1e30
PAGE_SIZE = 128

LANES = 128
SUBLANES = 8
VMEM_LIMIT_BYTES = 56 * 1024 * 1024

PROJ_ROWS = 256
OUT_ROWS = 512
FLASH_BLOCK = 512
FFN_CHUNK = 1024
PAGES_PER_STEP = 16


def _dot(a, b):
    return jnp.dot(a, b, preferred_element_type=F32)


def _dot_nt(a, b):
    return lax.dot_general(a, b, (((1,), (1,)), ((), ())), preferred_element_type=F32)


def _rms(x, g):
    return x * lax.rsqrt(jnp.mean(x * x, axis=-1, keepdims=True) + EPS) * g


def _full_spec(shape):
    nd = len(shape)
    return pl.BlockSpec(shape, lambda *_: (0,) * nd)


def _row_spec(rows, cols):
    return pl.BlockSpec((rows, cols), lambda i, *_: (i, 0))


def _params(*sem):
    return pltpu.CompilerParams(dimension_semantics=sem, vmem_limit_bytes=VMEM_LIMIT_BYTES)


def _s5_discretise_body(are_ref, aim_ref, ldt_ref, bre_ref, bim_ref, abre_ref, abim_ref, bbre_ref, bbim_ref):
    a_re = are_ref[...]
    a_im = aim_ref[...]
    dt = jnp.exp(ldt_ref[...])
    mag = jnp.exp(a_re * dt)
    ab_re = mag * jnp.cos(a_im * dt)
    ab_im = mag * jnp.sin(a_im * dt)
    den = a_re * a_re + a_im * a_im
    n_re = ab_re - 1.0
    n_im = ab_im
    k_re = (n_re * a_re + n_im * a_im) / den
    k_im = (n_im * a_re - n_re * a_im) / den
    b_re = bre_ref[...]
    b_im = bim_ref[...]
    abre_ref[...] = ab_re
    abim_ref[...] = ab_im
    bbre_ref[...] = k_re * b_re - k_im * b_im
    bbim_ref[...] = k_re * b_im + k_im * b_re


def _s5_discretise(a_re, a_im, log_dt, b_re, b_im):
    L, G, P, CH = b_re.shape
    rows = L * G * CH
    rep = lambda x: jnp.broadcast_to(x[:, :, None, :], (L, G, CH, P)).reshape(rows, P)
    a_re_r, a_im_r = rep(a_re), rep(a_im)
    ldt_r = jnp.broadcast_to(log_dt[:, :, None, None], (L, G, CH, P)).reshape(rows, P)
    bt_re = b_re.transpose(0, 1, 3, 2).reshape(rows, P)
    bt_im = b_im.transpose(0, 1, 3, 2).reshape(rows, P)
    out = jax.ShapeDtypeStruct((rows, P), F32)
    ab_re, ab_im, bb_re, bb_im = pl.pallas_call(
        _s5_discretise_body, out_shape=(out, out, out, out), name="s5_discretise")(
            a_re_r, a_im_r, ldt_r, bt_re, bt_im)
    lam = jnp.stack([ab_re.reshape(L, G, CH, P)[:, :, 0, :].reshape(L, G * P),
                     ab_im.reshape(L, G, CH, P)[:, :, 0, :].reshape(L, G * P)], axis=1)
    eye = jnp.eye(G, dtype=F32)

    def blockdiag(bb):
        bb = bb.reshape(L, G, CH, P)
        return (bb[:, :, :, None, :] * eye[None, :, None, :, None]).reshape(L, G * CH, G * P)

    bblk = jnp.concatenate([blockdiag(bb_re), blockdiag(bb_im)], axis=-1).astype(BF16)
    return lam, bblk


class _Dims:
    def __init__(self, w_in, w_uq, w_uk, w_uv, s5_b_re, gm_ws):
        self.d_model = w_in.shape[1]
        self.kv_lora = w_uk.shape[1]
        self.heads = w_uk.shape[2]
        self.nope = w_uk.shape[3]
        self.v_dim = w_uv.shape[3]
        self.q_lora = w_uq.shape[1]
        self.rope = w_uq.shape[2] // self.heads - self.nope
        self.s5_groups, self.s5_state, self.s5_ch = s5_b_re.shape[1:]
        self.s5_width = self.s5_groups * self.s5_ch
        self.s5_lanes = self.s5_groups * self.s5_state
        self.gm_heads = gm_ws.shape[1]
        self.chunk = gm_ws.shape[2]
        self.gm_width = (w_in.shape[2] - self.q_lora - self.kv_lora - self.rope - self.s5_width) // 2
        self.gm_head_dim = self.gm_width // self.gm_heads
        self.mla_width = self.heads * self.v_dim
        self.attn_scale = 1.0 / math.sqrt(self.nope + self.rope)
        assert self.rope + self.nope <= LANES and self.rope % 2 == 0
        self.qk_cols = self.heads * LANES
        self.o_cq = 0
        self.o_ckv = self.o_cq + self.q_lora
        self.o_s5 = self.o_ckv + self.kv_lora
        self.o_ugm = self.o_s5 + self.s5_width
        self.o_vgm = self.o_ugm + self.gm_width
        self.o_kra = self.o_vgm + self.gm_width
        self.o_krb = self.o_kra + LANES
        self.in_cols = self.o_krb + LANES
        for off in (self.o_ckv, self.o_s5, self.o_ugm, self.o_vgm, self.o_kra):
            assert off % LANES == 0


def _project_common(d, h_ref, ca_ref, sb_ref, gmix_ref, win_ref, gq_ref, wq_ref, gkv_ref,
                    q_ref, ckv_ref, kr_ref):
    a = _rms(h_ref[...], gmix_ref[...]).astype(BF16)
    z = _dot(a, win_ref[...])
    ca = ca_ref[...]
    sb = sb_ref[...]
    cqn = _rms(z[:, d.o_cq:d.o_cq + d.q_lora], gq_ref[...]).astype(BF16)
    qq = _dot(cqn, wq_ref[...])
    for hh in range(d.heads):
        lo = hh * LANES
        q_ref[:, lo:lo + LANES] = (
            (qq[:, lo:lo + LANES] * ca + qq[:, d.qk_cols + lo:d.qk_cols + lo + LANES] * sb)
            * d.attn_scale).astype(BF16)
    ckv = _rms(z[:, d.o_ckv:d.o_ckv + d.kv_lora], gkv_ref[...])
    ckv_ref[...] = ckv
    kr_slot = z[:, d.o_kra:d.o_kra + LANES] * ca + z[:, d.o_krb:d.o_krb + LANES] * sb
    kr_ref[...] = kr_slot[:, :d.rope]
    u_s5 = z[:, d.o_s5:d.o_s5 + d.s5_width]
    u_gm = z[:, d.o_ugm:d.o_ugm + d.gm_width]
    v_gm = z[:, d.o_vgm:d.o_vgm + d.gm_width]
    return q_ref, ckv, kr_slot, u_s5, u_gm, v_gm


def _s5_readout(d, hr_b, hi_b, u_s5, cblk_ref, dskip_ref, gluw_ref, glub_ref):
    n = d.s5_lanes
    y = _dot(hr_b, cblk_ref[0:n, :]) + _dot(hi_b, cblk_ref[n:2 * n, :]) + dskip_ref[...] * u_s5
    g = jax.nn.gelu(y)
    zz = _dot(g.astype(BF16), gluw_ref[...]) + glub_ref[...]
    w = d.s5_width
    return zz[:, :w] * jax.nn.sigmoid(zz[:, w:])


def _cmul_add(ar, ai, br, bi, cr, ci):
    return ar * br - ai * bi + cr, ar * bi + ai * br + ci


def _proj_prompt_body(d, rows,
                      h_ref, ca_ref, sb_ref, gmix_ref, win_ref, gq_ref, wq_ref, gkv_ref, wka_ref, wuvt_ref,
                      lam_ref, bblk_ref, cblk_ref, dskip_ref, gluw_ref, glub_ref, ggm_ref, gws_ref, gbias_ref,
                      q_ref, k_ref, vt_ref, ckv_ref, kr_ref, os5_ref, ogm_ref, sre_ref, sim_ref,
                      xr_scr, xi_scr, er_scr, ei_scr, cr_scr, ci_scr, st_scr):
    @pl.when(pl.program_id(0) == 0)
    def _():
        st_scr[...] = jnp.zeros_like(st_scr)

    _, ckv, kr_slot, u_s5, u_gm, v_gm = _project_common(
        d, h_ref, ca_ref, sb_ref, gmix_ref, win_ref, gq_ref, wq_ref, gkv_ref, q_ref, ckv_ref, kr_ref)

    ckvb = ckv.astype(BF16)
    kk = _dot(ckvb, wka_ref[...])
    for hh in range(d.heads):
        lo = hh * LANES
        k_ref[:, lo:lo + LANES] = (kk[:, lo:lo + LANES] + kr_slot).astype(BF16)
    vt_ref[...] = _dot_nt(wuvt_ref[...], ckvb).astype(BF16)

    n = d.s5_lanes
    bu = _dot(u_s5.astype(BF16), bblk_ref[...])
    groups = rows // SUBLANES
    for s in range(n // LANES):
        lo = s * LANES
        xr = xr_scr.at[s]
        xi = xi_scr.at[s]
        xr[...] = bu[:, lo:lo + LANES]
        xi[...] = bu[:, n + lo:n + lo + LANES]
        lam_r = lam_ref[0:1, lo:lo + LANES]
        lam_i = lam_ref[1:2, lo:lo + LANES]
        pw_r, pw_i = [lam_r], [lam_i]
        for _ in range(SUBLANES - 1):
            nr, ni = _cmul_add(pw_r[-1], pw_i[-1], lam_r, lam_i, 0.0, 0.0)
            pw_r.append(nr)
            pw_i.append(ni)
        hr = xr[pl.ds(0, groups, stride=SUBLANES), :]
        hi = xi[pl.ds(0, groups, stride=SUBLANES), :]
        for r in range(1, SUBLANES):
            hr, hi = _cmul_add(lam_r, lam_i, hr, hi,
                               xr[pl.ds(r, groups, stride=SUBLANES), :],
                               xi[pl.ds(r, groups, stride=SUBLANES), :])
            xr[pl.ds(r, groups, stride=SUBLANES), :] = hr
            xi[pl.ds(r, groups, stride=SUBLANES), :] = hi
        er_scr[...] = hr
        ei_scr[...] = hi
        cr = st_scr[0:1, lo:lo + LANES]
        ci = st_scr[1:2, lo:lo + LANES]
        for j in range(groups):
            cr_scr[j:j + 1, :] = cr
            ci_scr[j:j + 1, :] = ci
            cr, ci = _cmul_add(pw_r[-1], pw_i[-1], cr, ci, er_scr[j:j + 1, :], ei_scr[j:j + 1, :])
        st_scr[0:1, lo:lo + LANES] = cr
        st_scr[1:2, lo:lo + LANES] = ci
        cin_r = cr_scr[...]
        cin_i = ci_scr[...]
        for r in range(SUBLANES):
            hr, hi = _cmul_add(pw_r[r], pw_i[r], cin_r, cin_i,
                               xr[pl.ds(r, groups, stride=SUBLANES), :],
                               xi[pl.ds(r, groups, stride=SUBLANES), :])
            xr[pl.ds(r, groups, stride=SUBLANES), :] = hr
            xi[pl.ds(r, groups, stride=SUBLANES), :] = hi
    sre_ref[...] = st_scr[0:1, :]
    sim_ref[...] = st_scr[1:2, :]
    h_re = jnp.concatenate([xr_scr[s] for s in range(n // LANES)], axis=1).astype(BF16)
    h_im = jnp.concatenate([xi_scr[s] for s in range(n // LANES)], axis=1).astype(BF16)
    o_s5 = _s5_readout(d, h_re, h_im, u_s5,
                       cblk_ref, dskip_ref, gluw_ref, glub_ref)
    os5_ref[...] = o_s5.astype(BF16)

    vn = _rms(v_gm, ggm_ref[...])
    c = d.chunk
    tril = lax.broadcasted_iota(jnp.int32, (c, c), 1) <= lax.broadcasted_iota(jnp.int32, (c, c), 0)
    head_of_lane = lax.broadcasted_iota(jnp.int32, (c, d.gm_width), 1) // d.gm_head_dim
    w_low = [jnp.where(tril, gws_ref[g], 0.0).astype(BF16) for g in range(d.gm_heads)]
    for cc in range(rows // c):
        vc = vn[cc * c:(cc + 1) * c, :]
        s = gbias_ref[...]
        for g in range(d.gm_heads):
            s = s + _dot(w_low[g], jnp.where(head_of_lane == g, vc, 0.0).astype(BF16))
        ogm_ref[cc * c:(cc + 1) * c, :] = (u_gm[cc * c:(cc + 1) * c, :] * s).astype(BF16)


def _proj_sample_body(d,
                      h_ref, ca_ref, sb_ref, gmix_ref, win_ref, gq_ref, wq_ref, gkv_ref, wukp_ref,
                      lam_ref, bblk_ref, cblk_ref, dskip_ref, gluw_ref, glub_ref, ggm_ref, gw0_ref, gb0_ref,
                      h0r_ref, h0i_ref,
                      q_ref, qlat_ref, ckv_ref, kr_ref, os5_ref, ogm_ref, sre_ref, sim_ref, vn_ref):
    _, _, _, u_s5, u_gm, v_gm = _project_common(
        d, h_ref, ca_ref, sb_ref, gmix_ref, win_ref, gq_ref, wq_ref, gkv_ref, q_ref, ckv_ref, kr_ref)

    for hh in range(d.heads):
        qlat_ref[:, hh * d.kv_lora:(hh + 1) * d.kv_lora] = _dot(
            q_ref[:, hh * LANES:(hh + 1) * LANES], wukp_ref[hh]).astype(BF16)

    n = d.s5_lanes
    bu = _dot(u_s5.astype(BF16), bblk_ref[...])
    hr, hi = _cmul_add(lam_ref[0:1, :], lam_ref[1:2, :], h0r_ref[...], h0i_ref[...], bu[:, :n], bu[:, n:])
    sre_ref[...] = hr
    sim_ref[...] = hi
    o_s5 = _s5_readout(d, hr.astype(BF16), hi.astype(BF16), u_s5, cblk_ref, dskip_ref, gluw_ref, glub_ref)
    os5_ref[...] = o_s5.astype(BF16)

    vn = _rms(v_gm, ggm_ref[...])
    vn_ref[...] = vn
    ogm_ref[...] = (u_gm * (gw0_ref[...] * vn + gb0_ref[...])).astype(BF16)


def _flash_body(d, blk, qi_ref, ki_ref, q_ref, k_ref, vt_ref, o_ref, m_scr, l_scr, acc_scr):
    step = pl.program_id(0)
    qi = qi_ref[step]
    ki = ki_ref[step]

    @pl.when(ki == 0)
    def _():
        m_scr[...] = jnp.full_like(m_scr, NEG_INIT)
        l_scr[...] = jnp.zeros_like(l_scr)
        acc_scr[...] = jnp.zeros_like(acc_scr)

    def update(masked):
        if masked:
            keep = (lax.broadcasted_iota(jnp.int32, (blk, blk), 0)
                    <= lax.broadcasted_iota(jnp.int32, (blk, blk), 1))
        for hh in range(d.heads):
            lo = hh * LANES
            st = _dot_nt(k_ref[:, lo:lo + LANES], q_ref[:, lo:lo + LANES])
            if masked:
                st = jnp.where(keep, st, NEG_INIT)
            m_prev = m_scr[hh:hh + 1, :]
            m_new = jnp.maximum(m_prev, jnp.max(st, axis=0, keepdims=True))
            alpha = jnp.exp(m_prev - m_new)
            p = jnp.exp(st - m_new)
            l_scr[hh:hh + 1, :] = alpha * l_scr[hh:hh + 1, :] + jnp.sum(p, axis=0, keepdims=True)
            vlo = hh * d.v_dim
            acc_scr[vlo:vlo + d.v_dim, :] = alpha * acc_scr[vlo:vlo + d.v_dim, :] + _dot(
                vt_ref[vlo:vlo + d.v_dim, :], p.astype(BF16))
            m_scr[hh:hh + 1, :] = m_new

    @pl.when(ki < qi)
    def _():
        update(False)

    @pl.when(ki == qi)
    def _():
        update(True)
        for hh in range(d.heads):
            vlo = hh * d.v_dim
            acc_scr[vlo:vlo + d.v_dim, :] = acc_scr[vlo:vlo + d.v_dim, :] / l_scr[hh:hh + 1, :]
        o_ref[...] = acc_scr[...].T.astype(BF16)


def _flash_attention(d, q, k, vt):
    t = q.shape[0]
    blk = min(FLASH_BLOCK, t)
    assert t % blk == 0
    nb = t // blk
    qi = np.concatenate([np.full(i + 1, i, np.int32) for i in range(nb)])
    ki = np.concatenate([np.arange(i + 1, dtype=np.int32) for i in range(nb)])
    grid_spec = pltpu.PrefetchScalarGridSpec(
        num_scalar_prefetch=2,
        grid=(len(qi),),
        in_specs=[
            pl.BlockSpec((blk, d.qk_cols), lambda s, qi, ki: (qi[s], 0)),
            pl.BlockSpec((blk, d.qk_cols), lambda s, qi, ki: (ki[s], 0)),
            pl.BlockSpec((d.mla_width, blk), lambda s, qi, ki: (0, ki[s])),
        ],
        out_specs=pl.BlockSpec((blk, d.mla_width), lambda s, qi, ki: (qi[s], 0)),
        scratch_shapes=[
            pltpu.VMEM((d.heads, blk), F32),
            pltpu.VMEM((d.heads, blk), F32),
            pltpu.VMEM((d.mla_width, blk), F32),
        ],
    )
    return pl.pallas_call(
        functools.partial(_flash_body, d, blk),
        grid_spec=grid_spec,
        out_shape=jax.ShapeDtypeStruct((t, d.mla_width), BF16),
        compiler_params=_params("arbitrary"),
        name="prompt_flash_attention",
    )(jnp.asarray(qi), jnp.asarray(ki), q, k, vt)


def _paged_body(d, pages, pt_ref, qlat_ref, q_ref, ckvn_ref, krn_ref, *rest):
    kc_refs = rest[:pages]
    kp_refs = rest[pages:2 * pages]
    o_ref = rest[2 * pages]
    kcb_scr, kpb_scr, m_scr, l_scr, acc_scr = rest[2 * pages + 1:]
    c = pl.program_id(1)

    @pl.when(c == 0)
    def _():
        m_scr[...] = jnp.full_like(m_scr, NEG_INIT)
        l_scr[...] = jnp.zeros_like(l_scr)
        acc_scr[...] = jnp.zeros_like(acc_scr)

    for j in range(pages):
        kcb_scr[j * PAGE_SIZE:(j + 1) * PAGE_SIZE, :] = kc_refs[j][...].astype(BF16)
        kpb_scr[j * PAGE_SIZE:(j + 1) * PAGE_SIZE, :] = kp_refs[j][...].astype(BF16)
    ql = qlat_ref[...]
    qr = q_ref[:, 0:d.rope]
    s = _dot_nt(ql, kcb_scr[...]) + _dot_nt(qr, kpb_scr[...])
    m_prev = m_scr[...]
    m_new = jnp.maximum(m_prev, jnp.max(s, axis=-1, keepdims=True))
    corr = jnp.exp(m_prev - m_new)
    p = jnp.exp(s - m_new)
    l_scr[...] = l_scr[...] * corr + jnp.sum(p, axis=-1, keepdims=True)
    acc_scr[...] = acc_scr[...] * corr + _dot(p.astype(BF16), kcb_scr[...])
    m_scr[...] = m_new

    @pl.when(c == pl.num_programs(1) - 1)
    def _():
        cn = ckvn_ref[...]
        s_self = (jnp.sum(ql.astype(F32) * cn, axis=-1, keepdims=True)
                  + jnp.sum(qr.astype(F32) * krn_ref[...], axis=-1, keepdims=True))
        m_prev = m_scr[...]
        m_new = jnp.maximum(m_prev, s_self)
        corr = jnp.exp(m_prev - m_new)
        p = jnp.exp(s_self - m_new)
        l = l_scr[...] * corr + p
        acc = acc_scr[...] * corr + p * cn
        o_ref[...] = (acc / l).astype(BF16)


def _paged_attention(d, layer, page_table, q_lat, q, ckv_new, kr_new, cache_ckv, cache_krope):
    db, n_pages = page_table.shape
    pages = min(PAGES_PER_STEP, n_pages)
    assert n_pages % pages == 0 and cache_ckv.shape[2] == PAGE_SIZE

    def page_spec(width, j):
        return pl.BlockSpec((None, None, PAGE_SIZE, width),
                            lambda b, c, pt: (layer, pt[b, c * pages + j], 0, 0))

    grid_spec = pltpu.PrefetchScalarGridSpec(
        num_scalar_prefetch=1,
        grid=(db, n_pages // pages),
        in_specs=[
            pl.BlockSpec((None, d.heads, d.kv_lora), lambda b, c, pt: (b, 0, 0)),
            pl.BlockSpec((None, d.heads, LANES), lambda b, c, pt: (b, 0, 0)),
            pl.BlockSpec((None, 1, d.kv_lora), lambda b, c, pt: (b, 0, 0)),
            pl.BlockSpec((None, 1, d.rope), lambda b, c, pt: (b, 0, 0)),
        ] + [page_spec(d.kv_lora, j) for j in range(pages)]
          + [page_spec(d.rope, j) for j in range(pages)],
        out_specs=pl.BlockSpec((None, d.heads, d.kv_lora), lambda b, c, pt: (b, 0, 0)),
        scratch_shapes=[
            pltpu.VMEM((pages * PAGE_SIZE, d.kv_lora), BF16),
            pltpu.VMEM((pages * PAGE_SIZE, d.rope), BF16),
            pltpu.VMEM((d.heads, 1), F32),
            pltpu.VMEM((d.heads, 1), F32),
            pltpu.VMEM((d.heads, d.kv_lora), F32),
        ],
    )
    return pl.pallas_call(
        functools.partial(_paged_body, d, pages),
        grid_spec=grid_spec,
        out_shape=jax.ShapeDtypeStruct((db, d.heads, d.kv_lora), BF16),
        compiler_params=_params("arbitrary", "arbitrary"),
        name="decode_paged_attention",
    )(page_table,
      q_lat.reshape(db, d.heads, d.kv_lora), q.reshape(db, d.heads, LANES),
      ckv_new.reshape(db, 1, d.kv_lora), kr_new.reshape(db, 1, d.rope),
      *([cache_ckv] * pages), *([cache_krope] * pages))


def _out_body(d, latent_attn, final_norm, d_ff, *refs):
    refs = list(refs)
    h_ref, att_ref, os5_ref, ogm_ref, ple_ref = refs[:5]
    refs = refs[5:]
    if latent_attn:
        wuvb_ref = refs.pop(0)
    wout_ref, gffn_ref, wup_ref, wdown_ref, gple_ref, wple_ref, wpg_ref = refs[:7]
    refs = refs[7:]
    if final_norm:
        gfin_ref = refs.pop(0)
    out_ref, = refs

    if latent_attn:
        o_mla = _dot(att_ref[...], wuvb_ref[...]).astype(BF16)
    else:
        o_mla = att_ref[...]
    m0 = d.mla_width
    m1 = m0 + d.s5_width
    m2 = m1 + d.gm_width
    h = (h_ref[...] + _dot(o_mla, wout_ref[0:m0, :]) + _dot(os5_ref[...], wout_ref[m0:m1, :])
         + _dot(ogm_ref[...], wout_ref[m1:m2, :]))
    a = _rms(h, gffn_ref[...]).astype(BF16)
    fc = min(FFN_CHUNK, d_ff)
    ffn = None
    for cc in range(d_ff // fc):
        f = jnp.maximum(_dot(a, wup_ref[:, cc * fc:(cc + 1) * fc]), 0.0)
        part = _dot((f * f).astype(BF16), wdown_ref[cc * fc:(cc + 1) * fc, :])
        ffn = part if ffn is None else ffn + part
    h = h + ffn
    gate = jax.nn.sigmoid(_dot(_rms(h, gple_ref[...]).astype(BF16), wpg_ref[...]))
    h = h + _dot(ple_ref[...].astype(BF16), wple_ref[...]) * gate
    if final_norm:
        h = _rms(h, gfin_ref[...])
    out_ref[...] = h


def _out_layer(d, h, att, o_s5, o_gm, ple, wuv_big, lw, g_final):
    t, dm = h.shape
    rows = min(OUT_ROWS, t)
    assert t % rows == 0
    d_ff = lw["w_up"].shape[1]
    latent_attn = wuv_big is not None
    final_norm = g_final is not None
    def wspec(x):
        return pl.BlockSpec(x.shape, lambda i: (0,) * x.ndim, pipeline_mode=pl.Buffered(1))

    args = [h, att, o_s5, o_gm, ple]
    specs = [_row_spec(rows, dm), _row_spec(rows, att.shape[1]), _row_spec(rows, o_s5.shape[1]),
             _row_spec(rows, o_gm.shape[1]), _row_spec(rows, ple.shape[1])]
    weights = ([wuv_big] if latent_attn else []) + [
        lw["w_out"], lw["g_ffn"], lw["w_up"], lw["w_down"], lw["g_ple"], lw["w_ple"], lw["w_pg"]]
    if final_norm:
        weights.append(g_final)
    return pl.pallas_call(
        functools.partial(_out_body, d, latent_attn, final_norm, d_ff),
        grid=(t // rows,),
        in_specs=specs + [wspec(w) for w in weights],
        out_specs=_row_spec(rows, dm),
        out_shape=jax.ShapeDtypeStruct((t, dm), F32),
        compiler_params=_params("parallel"),
        name="out_ffn_ple",
    )(*args, *weights)


def _proj_prompt(d, h, ca, sb, lw):
    t, dm = h.shape
    rows = min(PROJ_ROWS, t)
    assert t % rows == 0 and rows % d.chunk == 0 and rows % SUBLANES == 0
    groups = rows // SUBLANES
    n = d.s5_lanes
    weights = [lw["g_mix"], lw["w_in"], lw["g_q"], lw["w_q"], lw["g_kv"], lw["w_ka"], lw["w_uvt"],
               lw["lam"], lw["bblk"], lw["cblk"], lw["s5_d"], lw["glu_w"], lw["glu_b"],
               lw["g_gm_v"], lw["gm_ws"], lw["gm_bias"]]
    out_shape = (
        jax.ShapeDtypeStruct((t, d.qk_cols), BF16),
        jax.ShapeDtypeStruct((t, d.qk_cols), BF16),
        jax.ShapeDtypeStruct((d.mla_width, t), BF16),
        jax.ShapeDtypeStruct((t, d.kv_lora), F32),
        jax.ShapeDtypeStruct((t, d.rope), F32),
        jax.ShapeDtypeStruct((t, d.s5_width), BF16),
        jax.ShapeDtypeStruct((t, d.gm_width), BF16),
        jax.ShapeDtypeStruct((1, n), F32),
        jax.ShapeDtypeStruct((1, n), F32),
    )
    out_specs = (
        _row_spec(rows, d.qk_cols), _row_spec(rows, d.qk_cols),
        pl.BlockSpec((d.mla_width, rows), lambda i: (0, i)),
        _row_spec(rows, d.kv_lora), _row_spec(rows, d.rope),
        _row_spec(rows, d.s5_width), _row_spec(rows, d.gm_width),
        _full_spec((1, n)), _full_spec((1, n)),
    )
    return pl.pallas_call(
        functools.partial(_proj_prompt_body, d, rows),
        grid=(t // rows,),
        in_specs=[_row_spec(rows, dm), _row_spec(rows, LANES), _row_spec(rows, LANES)]
                 + [_full_spec(w.shape) for w in weights],
        out_specs=out_specs,
        out_shape=out_shape,
        scratch_shapes=[
            pltpu.VMEM((n // LANES, rows, LANES), F32), pltpu.VMEM((n // LANES, rows, LANES), F32),
            pltpu.VMEM((groups, LANES), F32), pltpu.VMEM((groups, LANES), F32),
            pltpu.VMEM((groups, LANES), F32), pltpu.VMEM((groups, LANES), F32),
            pltpu.VMEM((2, n), F32),
        ],
        compiler_params=_params("arbitrary"),
        name="proj_mixers_prompt",
    )(h, ca, sb, *weights)


def _proj_sample(d, h, ca, sb, h0_re, h0_im, lw):
    db, dm = h.shape
    n = d.s5_lanes
    weights = [lw["g_mix"], lw["w_in"], lw["g_q"], lw["w_q"], lw["g_kv"], lw["w_ukp"],
               lw["lam"], lw["bblk"], lw["cblk"], lw["s5_d"], lw["glu_w"], lw["glu_b"],
               lw["g_gm_v"], lw["gm_w0"], lw["gm_b0"]]
    out_shape = (
        jax.ShapeDtypeStruct((db, d.qk_cols), BF16),
        jax.ShapeDtypeStruct((db, d.heads * d.kv_lora), BF16),
        jax.ShapeDtypeStruct((db, d.kv_lora), F32),
        jax.ShapeDtypeStruct((db, d.rope), F32),
        jax.ShapeDtypeStruct((db, d.s5_width), BF16),
        jax.ShapeDtypeStruct((db, d.gm_width), BF16),
        jax.ShapeDtypeStruct((db, n), F32),
        jax.ShapeDtypeStruct((db, n), F32),
        jax.ShapeDtypeStruct((db, d.gm_width), F32),
    )
    args = [h, ca, sb, *weights, h0_re, h0_im]
    return pl.pallas_call(
        functools.partial(_proj_sample_body, d),
        grid=(1,),
        in_specs=[_full_spec(a.shape) for a in args],
        out_specs=tuple(_full_spec(o.shape) for o in out_shape),
        out_shape=out_shape,
        compiler_params=_params("arbitrary"),
        name="proj_mixers_sample",
    )(*args)


def _rope_tables(d, positions):
    inv = ROPE_THETA ** (-jnp.arange(0, d.rope, 2, dtype=F32) / d.rope)
    ang = positions.astype(F32)[:, None] * inv[None, :]
    cos, sin = jnp.cos(ang), jnp.sin(ang)
    t = positions.shape[0]
    ca = jnp.concatenate([cos, cos, jnp.ones((t, d.nope), F32),
                          jnp.zeros((t, LANES - d.rope - d.nope), F32)], axis=1)
    sb = jnp.concatenate([sin, sin, jnp.zeros((t, LANES - d.rope), F32)], axis=1)
    return ca, sb


def _layer_weights(d, l, w, lam, bblk):
    f = {}
    r2 = d.rope // 2
    w_in = w["w_in"][l]
    o = np.cumsum([0, d.q_lora, d.kv_lora, d.rope, d.s5_width, d.gm_width, d.gm_width])
    c_q, c_kv, k_rope, u_s5, u_gm, v_gm = [w_in[:, o[i]:o[i + 1]] for i in range(6)]
    zpad = jnp.zeros((d.d_model, LANES - d.rope), F32)
    k_rot = jnp.concatenate([-k_rope[:, r2:], k_rope[:, :r2]], axis=1)
    f["w_in"] = jnp.concatenate([c_q, c_kv, u_s5, u_gm, v_gm, k_rope, zpad, k_rot, zpad], axis=1).astype(BF16)

    w_uq = w["w_uq"][l].reshape(d.q_lora, d.heads, d.nope + d.rope)
    qn, qr = w_uq[:, :, :d.nope], w_uq[:, :, d.nope:]
    qrot = jnp.concatenate([-qr[:, :, r2:], qr[:, :, :r2]], axis=2)
    hpad = jnp.zeros((d.q_lora, d.heads, LANES - d.rope - d.nope), F32)
    plain = jnp.concatenate([qr, qn, hpad], axis=2).reshape(d.q_lora, d.qk_cols)
    partner = jnp.concatenate([qrot, jnp.zeros_like(qn), hpad], axis=2).reshape(d.q_lora, d.qk_cols)
    f["w_q"] = jnp.concatenate([plain, partner], axis=1).astype(BF16)

    w_uk = w["w_uk"][l]
    kz0 = jnp.zeros((d.kv_lora, d.heads, d.rope), F32)
    kz1 = jnp.zeros((d.kv_lora, d.heads, LANES - d.rope - d.nope), F32)
    f["w_ka"] = jnp.concatenate([kz0, w_uk, kz1], axis=2).reshape(d.kv_lora, d.qk_cols).astype(BF16)
    f["w_ukp"] = jnp.concatenate([kz0, w_uk, kz1], axis=2).transpose(1, 2, 0).astype(BF16)
    w_uv = w["w_uv"][l]
    f["w_uvt"] = w_uv.reshape(d.kv_lora, d.mla_width).T.astype(BF16)
    eye_h = jnp.eye(d.heads, dtype=F32)
    f["w_uv_big"] = (w_uv.transpose(1, 0, 2)[:, :, None, :] * eye_h[:, None, :, None]).reshape(
        d.heads * d.kv_lora, d.mla_width).astype(BF16)

    f["lam"] = lam[l]
    f["bblk"] = bblk[l]
    eye_g = jnp.eye(d.s5_groups, dtype=F32)

    def cblock(cc):
        return (cc.transpose(0, 2, 1)[:, :, None, :] * eye_g[:, None, :, None]).reshape(d.s5_lanes, d.s5_width)

    f["cblk"] = jnp.concatenate([cblock(w["s5_c_re"][l]), -cblock(w["s5_c_im"][l])], axis=0).astype(BF16)
    f["s5_d"] = w["s5_d"][l][None, :]
    f["glu_w"] = w["s5_glu_w"][l].astype(BF16)
    f["glu_b"] = w["s5_glu_b"][l][None, :]

    f["g_gm_v"] = w["g_gm_v"][l][None, :]
    f["gm_ws"] = w["gm_ws"][l]
    gm_bs = w["gm_bs"][l]
    f["gm_bias"] = jnp.repeat(gm_bs.T, d.gm_head_dim, axis=1)
    f["gm_w0"] = jnp.repeat(w["gm_ws"][l][:, 0, 0], d.gm_head_dim)[None, :]
    f["gm_b0"] = jnp.repeat(gm_bs[:, 0], d.gm_head_dim)[None, :]

    for name in ("g_mix", "g_q", "g_kv", "g_ffn", "g_ple"):
        f[name] = w[name][l][None, :]
    for name in ("w_out", "w_up", "w_down", "w_ple", "w_pg"):
        f[name] = w[name][l].astype(BF16)
    return f


def kernel(x_prompt, x_sample, cache_ckv, cache_krope, state_s5_re, state_s5_im, page_table, p_prompt, p_sample, g_mix, w_in, g_q, w_uq, g_kv, w_uk, w_uv, s5_a_re, s5_a_im, s5_log_dt, s5_b_re, s5_b_im, s5_c_re, s5_c_im, s5_d, s5_glu_w, s5_glu_b, g_gm_v, gm_ws, gm_bs, w_out, g_ffn, w_up, w_down, g_ple, w_ple, w_pg, g_final):
    w = dict(g_mix=g_mix, w_in=w_in, g_q=g_q, w_uq=w_uq, g_kv=g_kv, w_uk=w_uk, w_uv=w_uv,
             s5_c_re=s5_c_re, s5_c_im=s5_c_im, s5_d=s5_d, s5_glu_w=s5_glu_w, s5_glu_b=s5_glu_b,
             g_gm_v=g_gm_v, gm_ws=gm_ws, gm_bs=gm_bs, w_out=w_out, g_ffn=g_ffn, w_up=w_up,
             w_down=w_down, g_ple=g_ple, w_ple=w_ple, w_pg=w_pg)
    d = _Dims(w_in, w_uq, w_uk, w_uv, s5_b_re, gm_ws)
    depth = w_in.shape[0]
    batch, seq, dm = x_prompt.shape
    db, dec_seq, _ = x_sample.shape
    assert batch == 1 and dec_seq == 1
    past_len = page_table.shape[1] * cache_ckv.shape[2]

    lam, bblk = _s5_discretise(s5_a_re, s5_a_im, s5_log_dt, s5_b_re, s5_b_im)
    ca_p, sb_p = _rope_tables(d, jnp.arange(seq))
    ca_s, sb_s = _rope_tables(d, jnp.full((db,), past_len))
    g_fin = g_final[None, :]

    hp = x_prompt.reshape(seq, dm)
    hs = x_sample.reshape(db, dm)
    outs = [[] for _ in range(9)]
    for l in range(depth):
        lw = _layer_weights(d, l, w, lam, bblk)
        last = l == depth - 1
        q, k, vt, ckv_p, kr_p, os5, ogm, sre_p, sim_p = _proj_prompt(d, hp, ca_p, sb_p, lw)
        att = _flash_attention(d, q, k, vt)
        hp = _out_layer(d, hp, att, os5, ogm, p_prompt[l].reshape(seq, -1), None, lw, g_fin if last else None)
        qs, qlat, ckv_s, kr_s, os5_s, ogm_s, sre_s, sim_s, vn_s = _proj_sample(
            d, hs, ca_s, sb_s, state_s5_re[l].reshape(db, -1), state_s5_im[l].reshape(db, -1), lw)
        olat = _paged_attention(d, l, page_table, qlat, qs, ckv_s, kr_s, cache_ckv, cache_krope)
        hs = _out_layer(d, hs, olat.reshape(db, -1), os5_s, ogm_s, p_sample[l].reshape(db, -1),
                        lw["w_uv_big"], lw, g_fin if last else None)
        gshape = (d.s5_groups, d.s5_state)
        for lst, val in zip(outs, (
                ckv_p.reshape(batch, seq, -1), kr_p.reshape(batch, seq, -1),
                ckv_s.reshape(db, dec_seq, -1), kr_s.reshape(db, dec_seq, -1),
                sre_p.reshape(batch, *gshape), sim_p.reshape(batch, *gshape),
                sre_s.reshape(db, *gshape), sim_s.reshape(db, *gshape),
                vn_s.reshape(db, dec_seq, -1))):
            lst.append(val)
    return (hp.reshape(batch, seq, dm), hs.reshape(db, dec_seq, dm), *[jnp.stack(o) for o in outs])
```

```python
import functools
import math

import numpy as np
import jax
import jax.numpy as jnp
from jax import lax
from jax.experimental import pallas as pl
from jax.experimental.pallas import tpu as pltpu

F32 = jnp.float32
BF16 = jnp.bfloat16

EPS = 1e-6
ROPE_THETA = 10000.0
NEG_INIT = -1e30
PAGE_SIZE = 128

LANES = 128
SUBLANES = 8
BF16_SUBLANES = 16
VMEM_LIMIT_BYTES = 56 * 1024 * 1024

PROJ_ROWS = 256
OUT_ROWS = 512
FLASH_BLOCK = 512
FFN_CHUNK = 1024
PAGES_PER_STEP = 16


def _dot(a, b):
    return jnp.dot(a, b, preferred_element_type=F32)


def _dot_nt(a, b):
    return lax.dot_general(a, b, (((1,), (1,)), ((), ())), preferred_element_type=F32)


def _rms(x, g):
    return x * lax.rsqrt(jnp.mean(x * x, axis=-1, keepdims=True) + EPS) * g


def _full_spec(shape):
    nd = len(shape)
    return pl.BlockSpec(shape, lambda *_: (0,) * nd)


def _row_spec(rows, cols):
    return pl.BlockSpec((rows, cols), lambda i, *_: (i, 0))


def _params(*sem):
    return pltpu.CompilerParams(dimension_semantics=sem, vmem_limit_bytes=VMEM_LIMIT_BYTES)


def _s5_discretise_body(are_ref, aim_ref, ldt_ref, bre_ref, bim_ref, abre_ref, abim_ref, bbre_ref, bbim_ref):
    a_re = are_ref[...]
    a_im = aim_ref[...]
    dt = jnp.exp(ldt_ref[...])
    mag = jnp.exp(a_re * dt)
    ab_re = mag * jnp.cos(a_im * dt)
    ab_im = mag * jnp.sin(a_im * dt)
    den = a_re * a_re + a_im * a_im
    n_re = ab_re - 1.0
    n_im = ab_im
    k_re = (n_re * a_re + n_im * a_im) / den
    k_im = (n_im * a_re - n_re * a_im) / den
    b_re = bre_ref[...]
    b_im = bim_ref[...]
    abre_ref[...] = ab_re
    abim_ref[...] = ab_im
    bbre_ref[...] = k_re * b_re - k_im * b_im
    bbim_ref[...] = k_re * b_im + k_im * b_re


def _s5_discretise(a_re, a_im, log_dt, b_re, b_im):
    L, G, P, CH = b_re.shape
    rows = L * G * CH
    rep = lambda x: jnp.broadcast_to(x[:, :, None, :], (L, G, CH, P)).reshape(rows, P)
    a_re_r, a_im_r = rep(a_re), rep(a_im)
    ldt_r = jnp.broadcast_to(log_dt[:, :, None, None], (L, G, CH, P)).reshape(rows, P)
    bt_re = b_re.transpose(0, 1, 3, 2).reshape(rows, P)
    bt_im = b_im.transpose(0, 1, 3, 2).reshape(rows, P)
    out = jax.ShapeDtypeStruct((rows, P), F32)
    ab_re, ab_im, bb_re, bb_im = pl.pallas_call(
        _s5_discretise_body, out_shape=(out, out, out, out), name="s5_discretise")(
            a_re_r, a_im_r, ldt_r, bt_re, bt_im)
    lam = jnp.stack([ab_re.reshape(L, G, CH, P)[:, :, 0, :].reshape(L, G * P),
                     ab_im.reshape(L, G, CH, P)[:, :, 0, :].reshape(L, G * P)], axis=1)
    eye = jnp.eye(G, dtype=F32)

    def blockdiag(bb):
        bb = bb.reshape(L, G, CH, P)
        return (bb[:, :, :, None, :] * eye[None, :, None, :, None]).reshape(L, G * CH, G * P)

    bblk = jnp.concatenate([blockdiag(bb_re), blockdiag(bb_im)], axis=-1).astype(BF16)
    return lam, bblk


class _Dims:
    def __init__(self, w_in, w_uq, w_uk, w_uv, s5_b_re, gm_ws):
        self.d_model = w_in.shape[1]
        self.kv_lora = w_uk.shape[1]
        self.heads = w_uk.shape[2]
        self.nope = w_uk.shape[3]
        self.v_dim = w_uv.shape[3]
        self.q_lora = w_uq.shape[1]
        self.rope = w_uq.shape[2] // self.heads - self.nope
        self.s5_groups, self.s5_state, self.s5_ch = s5_b_re.shape[1:]
        self.s5_width = self.s5_groups * self.s5_ch
        self.s5_lanes = self.s5_groups * self.s5_state
        self.gm_heads = gm_ws.shape[1]
        self.chunk = gm_ws.shape[2]
        self.gm_width = (w_in.shape[2] - self.q_lora - self.kv_lora - self.rope - self.s5_width) // 2
        self.gm_head_dim = self.gm_width // self.gm_heads
        self.mla_width = self.heads * self.v_dim
        self.v_slab = self.v_dim + BF16_SUBLANES
        self.attn_scale = 1.0 / math.sqrt(self.nope + self.rope)
        assert self.rope + self.nope <= LANES and self.rope % 2 == 0
        self.qk_cols = self.heads * LANES
        self.o_cq = 0
        self.o_ckv = self.o_cq + self.q_lora
        self.o_s5 = self.o_ckv + self.kv_lora
        self.o_ugm = self.o_s5 + self.s5_width
        self.o_vgm = self.o_ugm + self.gm_width
        self.o_kra = self.o_vgm + self.gm_width
        self.o_krb = self.o_kra + LANES
        self.in_cols = self.o_krb + LANES
        for off in (self.o_ckv, self.o_s5, self.o_ugm, self.o_vgm, self.o_kra):
            assert off % LANES == 0


def _project_common(d, q_scale, h_ref, ca_ref, sb_ref, gmix_ref, win_ref, gq_ref, wq_ref, gkv_ref,
                    q_ref, ckv_ref, kr_ref):
    a = _rms(h_ref[...], gmix_ref[...]).astype(BF16)
    z = _dot(a, win_ref[...])
    ca = ca_ref[...]
    sb = sb_ref[...]
    cqn = _rms(z[:, d.o_cq:d.o_cq + d.q_lora], gq_ref[...]).astype(BF16)
    qq = _dot(cqn, wq_ref[...])
    for hh in range(d.heads):
        lo = hh * LANES
        q_ref[:, lo:lo + LANES] = (
            (qq[:, lo:lo + LANES] * ca + qq[:, d.qk_cols + lo:d.qk_cols + lo + LANES] * sb)
            * q_scale).astype(BF16)
    ckv = _rms(z[:, d.o_ckv:d.o_ckv + d.kv_lora], gkv_ref[...])
    ckv_ref[...] = ckv
    kr_slot = z[:, d.o_kra:d.o_kra + LANES] * ca + z[:, d.o_krb:d.o_krb + LANES] * sb
    kr_ref[...] = kr_slot[:, :d.rope]
    u_s5 = z[:, d.o_s5:d.o_s5 + d.s5_width]
    u_gm = z[:, d.o_ugm:d.o_ugm + d.gm_width]
    v_gm = z[:, d.o_vgm:d.o_vgm + d.gm_width]
    return q_ref, ckv, kr_slot, u_s5, u_gm, v_gm


def _s5_readout(d, hr_b, hi_b, u_s5, cblk_ref, dskip_ref, gluw_ref, glub_ref):
    n = d.s5_lanes
    y = _dot(hr_b, cblk_ref[0:n, :]) + _dot(hi_b, cblk_ref[n:2 * n, :]) + dskip_ref[...] * u_s5
    g = jax.nn.gelu(y)
    zz = _dot(g.astype(BF16), gluw_ref[...]) + glub_ref[...]
    w = d.s5_width
    return zz[:, :w] * jax.nn.sigmoid(zz[:, w:])


def _cmul_add(ar, ai, br, bi, cr, ci):
    return ar * br - ai * bi + cr, ar * bi + ai * br + ci


def _proj_prompt_body(d, rows,
                      h_ref, ca_ref, sb_ref, gmix_ref, win_ref, gq_ref, wq_ref, gkv_ref, wka_ref, wuvt_ref,
                      lam_ref, bblk_ref, cblk_ref, dskip_ref, gluw_ref, glub_ref, ggm_ref, gws_ref, gbias_ref,
                      q_ref, k_ref, vt_ref, ckv_ref, kr_ref, os5_ref, ogm_ref, sre_ref, sim_ref,
                      xr_scr, xi_scr, er_scr, ei_scr, cr_scr, ci_scr, st_scr):
    @pl.when(pl.program_id(0) == 0)
    def _():
        st_scr[...] = jnp.zeros_like(st_scr)

    _, ckv, kr_slot, u_s5, u_gm, v_gm = _project_common(
        d, d.attn_scale * math.log2(math.e),
        h_ref, ca_ref, sb_ref, gmix_ref, win_ref, gq_ref, wq_ref, gkv_ref, q_ref, ckv_ref, kr_ref)

    ckvb = ckv.astype(BF16)
    kk = _dot(ckvb, wka_ref[...])
    for hh in range(d.heads):
        lo = hh * LANES
        k_ref[:, lo:lo + LANES] = (kk[:, lo:lo + LANES] + kr_slot).astype(BF16)
    vt = _dot_nt(wuvt_ref[...], ckvb).astype(BF16)
    ones_tile = (lax.broadcasted_iota(jnp.int32, (BF16_SUBLANES, rows), 0) == 0).astype(BF16)
    for hh in range(d.heads):
        vlo = hh * d.v_slab
        vt_ref[vlo:vlo + d.v_dim, :] = vt[hh * d.v_dim:(hh + 1) * d.v_dim, :]
        vt_ref[vlo + d.v_dim:vlo + d.v_slab, :] = ones_tile

    n = d.s5_lanes
    bu = _dot(u_s5.astype(BF16), bblk_ref[...])
    groups = rows // SUBLANES
    for s in range(n // LANES):
        lo = s * LANES
        xr = xr_scr.at[s]
        xi = xi_scr.at[s]
        xr[...] = bu[:, lo:lo + LANES]
        xi[...] = bu[:, n + lo:n + lo + LANES]
        lam_r = lam_ref[0:1, lo:lo + LANES]
        lam_i = lam_ref[1:2, lo:lo + LANES]
        pw_r, pw_i = [lam_r], [lam_i]
        for _ in range(SUBLANES - 1):
            nr, ni = _cmul_add(pw_r[-1], pw_i[-1], lam_r, lam_i, 0.0, 0.0)
            pw_r.append(nr)
            pw_i.append(ni)
        hr = xr[pl.ds(0, groups, stride=SUBLANES), :]
        hi = xi[pl.ds(0, groups, stride=SUBLANES), :]
        for r in range(1, SUBLANES):
            hr, hi = _cmul_add(lam_r, lam_i, hr, hi,
                               xr[pl.ds(r, groups, stride=SUBLANES), :],
                               xi[pl.ds(r, groups, stride=SUBLANES), :])
            xr[pl.ds(r, groups, stride=SUBLANES), :] = hr
            xi[pl.ds(r, groups, stride=SUBLANES), :] = hi
        er_scr[...] = hr
        ei_scr[...] = hi
        cr = st_scr[0:1, lo:lo + LANES]
        ci = st_scr[1:2, lo:lo + LANES]
        for j in range(groups):
            cr_scr[j:j + 1, :] = cr
            ci_scr[j:j + 1, :] = ci
            cr, ci = _cmul_add(pw_r[-1], pw_i[-1], cr, ci, er_scr[j:j + 1, :], ei_scr[j:j + 1, :])
        st_scr[0:1, lo:lo + LANES] = cr
        st_scr[1:2, lo:lo + LANES] = ci
        cin_r = cr_scr[...]
        cin_i = ci_scr[...]
        for r in range(SUBLANES):
            hr, hi = _cmul_add(pw_r[r], pw_i[r], cin_r, cin_i,
                               xr[pl.ds(r, groups, stride=SUBLANES), :],
                               xi[pl.ds(r, groups, stride=SUBLANES), :])
            xr[pl.ds(r, groups, stride=SUBLANES), :] = hr
            xi[pl.ds(r, groups, stride=SUBLANES), :] = hi
    sre_ref[...] = st_scr[0:1, :]
    sim_ref[...] = st_scr[1:2, :]
    h_re = jnp.concatenate([xr_scr[s] for s in range(n // LANES)], axis=1).astype(BF16)
    h_im = jnp.concatenate([xi_scr[s] for s in range(n // LANES)], axis=1).astype(BF16)
    o_s5 = _s5_readout(d, h_re, h_im, u_s5,
                       cblk_ref, dskip_ref, gluw_ref, glub_ref)
    os5_ref[...] = o_s5.astype(BF16)

    vn = _rms(v_gm, ggm_ref[...])
    c = d.chunk
    tril = lax.broadcasted_iota(jnp.int32, (c, c), 1) <= lax.broadcasted_iota(jnp.int32, (c, c), 0)
    head_of_lane = lax.broadcasted_iota(jnp.int32, (c, d.gm_width), 1) // d.gm_head_dim
    w_low = [jnp.where(tril, gws_ref[g], 0.0).astype(BF16) for g in range(d.gm_heads)]
    for cc in range(rows // c):
        vc = vn[cc * c:(cc + 1) * c, :]
        s = gbias_ref[...]
        for g in range(d.gm_heads):
            s = s + _dot(w_low[g], jnp.where(head_of_lane == g, vc, 0.0).astype(BF16))
        ogm_ref[cc * c:(cc + 1) * c, :] = (u_gm[cc * c:(cc + 1) * c, :] * s).astype(BF16)


def _proj_sample_body(d,
                      h_ref, ca_ref, sb_ref, gmix_ref, win_ref, gq_ref, wq_ref, gkv_ref, wukp_ref,
                      lam_ref, bblk_ref, cblk_ref, dskip_ref, gluw_ref, glub_ref, ggm_ref, gw0_ref, gb0_ref,
                      h0r_ref, h0i_ref,
                      q_ref, qlat_ref, ckv_ref, kr_ref, os5_ref, ogm_ref, sre_ref, sim_ref, vn_ref):
    _, _, _, u_s5, u_gm, v_gm = _project_common(
        d, d.attn_scale,
        h_ref, ca_ref, sb_ref, gmix_ref, win_ref, gq_ref, wq_ref, gkv_ref, q_ref, ckv_ref, kr_ref)

    for hh in range(d.heads):
        qlat_ref[:, hh * d.kv_lora:(hh + 1) * d.kv_lora] = _dot(
            q_ref[:, hh * LANES:(hh + 1) * LANES], wukp_ref[hh]).astype(BF16)

    n = d.s5_lanes
    bu = _dot(u_s5.astype(BF16), bblk_ref[...])
    hr, hi = _cmul_add(lam_ref[0:1, :], lam_ref[1:2, :], h0r_ref[...], h0i_ref[...], bu[:, :n], bu[:, n:])
    sre_ref[...] = hr
    sim_ref[...] = hi
    o_s5 = _s5_readout(d, hr.astype(BF16), hi.astype(BF16), u_s5, cblk_ref, dskip_ref, gluw_ref, glub_ref)
    os5_ref[...] = o_s5.astype(BF16)

    vn = _rms(v_gm, ggm_ref[...])
    vn_ref[...] = vn
    ogm_ref[...] = (u_gm * (gw0_ref[...] * vn + gb0_ref[...])).astype(BF16)


def _flash_body(d, blk, qi_ref, ki_ref, q_ref, k_ref, vt_ref, o_ref, m_scr, acc_scr, s_scr):
    step = pl.program_id(0)
    qi = qi_ref[step]
    ki = ki_ref[step]

    @pl.when(ki == 0)
    def _():
        m_scr[...] = jnp.full_like(m_scr, NEG_INIT)
        acc_scr[...] = jnp.zeros_like(acc_scr)

    def scores(hh):
        lo = hh * LANES
        s_scr[hh % 2] = _dot_nt(k_ref[:, lo:lo + LANES], q_ref[:, lo:lo + LANES])

    def update(masked):
        if masked:
            keep = (lax.broadcasted_iota(jnp.int32, (blk, blk), 0)
                    <= lax.broadcasted_iota(jnp.int32, (blk, blk), 1))
        scores(0)
        for hh in range(d.heads):
            if hh + 1 < d.heads:
                scores(hh + 1)
            st = s_scr[hh % 2]
            if masked:
                st = jnp.where(keep, st, NEG_INIT)
            m_prev = m_scr[hh:hh + 1, :]
            m_new = jnp.maximum(m_prev, jnp.max(st, axis=0, keepdims=True))
            alpha = jnp.exp2(m_prev - m_new)
            p = jnp.exp2(st - m_new).astype(BF16)
            vlo = hh * d.v_slab
            acc_scr[vlo:vlo + d.v_slab, :] = alpha * acc_scr[vlo:vlo + d.v_slab, :] + _dot(
                vt_ref[vlo:vlo + d.v_slab, :], p)
            m_scr[hh:hh + 1, :] = m_new

    @pl.when(ki < qi)
    def _():
        update(False)

    @pl.when(ki == qi)
    def _():
        update(True)
        outs = []
        for hh in range(d.heads):
            vlo = hh * d.v_slab
            outs.append(acc_scr[vlo:vlo + d.v_dim, :] / acc_scr[vlo + d.v_dim:vlo + d.v_dim + 1, :])
        o_ref[...] = jnp.concatenate(outs, axis=0).T.astype(BF16)


def _flash_attention(d, q, k, vt):
    t = q.shape[0]
    blk = min(FLASH_BLOCK, t)
    assert t % blk == 0
    nb = t // blk
    qi = np.concatenate([np.full(i + 1, i, np.int32) for i in range(nb)])
    ki = np.concatenate([np.arange(i + 1, dtype=np.int32) for i in range(nb)])
    grid_spec = pltpu.PrefetchScalarGridSpec(
        num_scalar_prefetch=2,
        grid=(len(qi),),
        in_specs=[
            pl.BlockSpec((blk, d.qk_cols), lambda s, qi, ki: (qi[s], 0)),
            pl.BlockSpec((blk, d.qk_cols), lambda s, qi, ki: (ki[s], 0)),
            pl.BlockSpec((d.heads * d.v_slab, blk), lambda s, qi, ki: (0, ki[s])),
        ],
        out_specs=pl.BlockSpec((blk, d.mla_width), lambda s, qi, ki: (qi[s], 0)),
        scratch_shapes=[
            pltpu.VMEM((d.heads, blk), F32),
            pltpu.VMEM((d.heads * d.v_slab, blk), F32),
            pltpu.VMEM((2, blk, blk), F32),
        ],
    )
    return pl.pallas_call(
        functools.partial(_flash_body, d, blk),
        grid_spec=grid_spec,
        out_shape=jax.ShapeDtypeStruct((t, d.mla_width), BF16),
        compiler_params=_params("arbitrary"),
        name="prompt_flash_attention",
    )(jnp.asarray(qi), jnp.asarray(ki), q, k, vt)


def _paged_body(d, pages, pt_ref, qlat_ref, q_ref, ckvn_ref, krn_ref, *rest):
    kc_refs = rest[:pages]
    kp_refs = rest[pages:2 * pages]
    o_ref = rest[2 * pages]
    kcb_scr, kpb_scr, m_scr, l_scr, acc_scr = rest[2 * pages + 1:]
    c = pl.program_id(1)

    @pl.when(c == 0)
    def _():
        m_scr[...] = jnp.full_like(m_scr, NEG_INIT)
        l_scr[...] = jnp.zeros_like(l_scr)
        acc_scr[...] = jnp.zeros_like(acc_scr)

    for j in range(pages):
        kcb_scr[j * PAGE_SIZE:(j + 1) * PAGE_SIZE, :] = kc_refs[j][...].astype(BF16)
        kpb_scr[:, j * PAGE_SIZE:(j + 1) * PAGE_SIZE] = kp_refs[j][...].astype(BF16)
    ql = qlat_ref[...]
    qr = q_ref[:, 0:d.rope]
    s = _dot_nt(ql, kcb_scr[...]) + _dot(qr, kpb_scr[...])
    m_prev = m_scr[...]
    m_new = jnp.maximum(m_prev, jnp.max(s, axis=-1, keepdims=True))
    corr = jnp.exp(m_prev - m_new)
    p = jnp.exp(s - m_new)
    l_scr[...] = l_scr[...] * corr + jnp.sum(p, axis=-1, keepdims=True)
    acc_scr[...] = acc_scr[...] * corr + _dot(p.astype(BF16), kcb_scr[...])
    m_scr[...] = m_new

    @pl.when(c == pl.num_programs(1) - 1)
    def _():
        cn = ckvn_ref[...]
        s_self = (jnp.sum(ql.astype(F32) * cn, axis=-1, keepdims=True)
                  + jnp.sum(qr.astype(F32) * krn_ref[...], axis=-1, keepdims=True))
        m_prev = m_scr[...]
        m_new = jnp.maximum(m_prev, s_self)
        corr = jnp.exp(m_prev - m_new)
        p = jnp.exp(s_self - m_new)
        l = l_scr[...] * corr + p
        acc = acc_scr[...] * corr + p * cn
        o_ref[...] = (acc / l).astype(BF16)


def _paged_attention(d, layer, page_table, q_lat, q, ckv_new, kr_new, cache_ckv, cache_krope_t):
    db, n_pages = page_table.shape
    pages = min(PAGES_PER_STEP, n_pages)
    assert n_pages % pages == 0 and cache_ckv.shape[2] == PAGE_SIZE

    def page_spec(rows, cols, j):
        return pl.BlockSpec((None, None, rows, cols),
                            lambda b, c, pt: (layer, pt[b, c * pages + j], 0, 0))

    grid_spec = pltpu.PrefetchScalarGridSpec(
        num_scalar_prefetch=1,
        grid=(db, n_pages // pages),
        in_specs=[
            pl.BlockSpec((None, d.heads, d.kv_lora), lambda b, c, pt: (b, 0, 0)),
            pl.BlockSpec((None, d.heads, LANES), lambda b, c, pt: (b, 0, 0)),
            pl.BlockSpec((None, 1, d.kv_lora), lambda b, c, pt: (b, 0, 0)),
            pl.BlockSpec((None, 1, d.rope), lambda b, c, pt: (b, 0, 0)),
        ] + [page_spec(PAGE_SIZE, d.kv_lora, j) for j in range(pages)]
          + [page_spec(d.rope, PAGE_SIZE, j) for j in range(pages)],
        out_specs=pl.BlockSpec((None, d.heads, d.kv_lora), lambda b, c, pt: (b, 0, 0)),
        scratch_shapes=[
            pltpu.VMEM((pages * PAGE_SIZE, d.kv_lora), BF16),
            pltpu.VMEM((d.rope, pages * PAGE_SIZE), BF16),
            pltpu.VMEM((d.heads, 1), F32),
            pltpu.VMEM((d.heads, 1), F32),
            pltpu.VMEM((d.heads, d.kv_lora), F32),
        ],
    )
    return pl.pallas_call(
        functools.partial(_paged_body, d, pages),
        grid_spec=grid_spec,
        out_shape=jax.ShapeDtypeStruct((db, d.heads, d.kv_lora), BF16),
        compiler_params=_params("arbitrary", "arbitrary"),
        name="decode_paged_attention",
    )(page_table,
      q_lat.reshape(db, d.heads, d.kv_lora), q.reshape(db, d.heads, LANES),
      ckv_new.reshape(db, 1, d.kv_lora), kr_new.reshape(db, 1, d.rope),
      *([cache_ckv] * pages), *([cache_krope_t] * pages))


def _out_body(d, latent_attn, final_norm, d_ff, *refs):
    refs = list(refs)
    h_ref, att_ref, os5_ref, ogm_ref, ple_ref = refs[:5]
    refs = refs[5:]
    if latent_attn:
        wuvb_ref = refs.pop(0)
    wout_ref, gffn_ref, wup_ref, wdown_ref, gple_ref, wple_ref, wpg_ref = refs[:7]
    refs = refs[7:]
    if final_norm:
        gfin_ref = refs.pop(0)
    out_ref, = refs

    if latent_attn:
        o_mla = _dot(att_ref[...], wuvb_ref[...]).astype(BF16)
    else:
        o_mla = att_ref[...]
    m0 = d.mla_width
    m1 = m0 + d.s5_width
    m2 = m1 + d.gm_width
    h = (h_ref[...] + _dot(o_mla, wout_ref[0:m0, :]) + _dot(os5_ref[...], wout_ref[m0:m1, :])
         + _dot(ogm_ref[...], wout_ref[m1:m2, :]))
    a = _rms(h, gffn_ref[...]).astype(BF16)
    fc = min(FFN_CHUNK, d_ff)
    ffn = None
    for cc in range(d_ff // fc):
        f = jnp.maximum(_dot(a, wup_ref[:, cc * fc:(cc + 1) * fc]), 0.0)
        part = _dot((f * f).astype(BF16), wdown_ref[cc * fc:(cc + 1) * fc, :])
        ffn = part if ffn is None else ffn + part
    h = h + ffn
    gate = jax.nn.sigmoid(_dot(_rms(h, gple_ref[...]).astype(BF16), wpg_ref[...]))
    h = h + _dot(ple_ref[...].astype(BF16), wple_ref[...]) * gate
    if final_norm:
        h = _rms(h, gfin_ref[...])
    out_ref[...] = h


def _out_layer(d, h, att, o_s5, o_gm, ple, wuv_big, lw, g_final):
    t, dm = h.shape
    rows = min(OUT_ROWS, t)
    assert t % rows == 0
    d_ff = lw["w_up"].shape[1]
    latent_attn = wuv_big is not None
    final_norm = g_final is not None
    def wspec(x):
        return pl.BlockSpec(x.shape, lambda i: (0,) * x.ndim, pipeline_mode=pl.Buffered(1))

    args = [h, att, o_s5, o_gm, ple]
    specs = [_row_spec(rows, dm), _row_spec(rows, att.shape[1]), _row_spec(rows, o_s5.shape[1]),
             _row_spec(rows, o_gm.shape[1]), _row_spec(rows, ple.shape[1])]
    weights = ([wuv_big] if latent_attn else []) + [
        lw["w_out"], lw["g_ffn"], lw["w_up"], lw["w_down"], lw["g_ple"], lw["w_ple"], lw["w_pg"]]
    if final_norm:
        weights.append(g_final)
    return pl.pallas_call(
        functools.partial(_out_body, d, latent_attn, final_norm, d_ff),
        grid=(t // rows,),
        in_specs=specs + [wspec(w) for w in weights],
        out_specs=_row_spec(rows, dm),
        out_shape=jax.ShapeDtypeStruct((t, dm), F32),
        compiler_params=_params("parallel"),
        name="out_ffn_ple",
    )(*args, *weights)


def _proj_prompt(d, h, ca, sb, lw):
    t, dm = h.shape
    rows = min(PROJ_ROWS, t)
    assert t % rows == 0 and rows % d.chunk == 0 and rows % SUBLANES == 0
    groups = rows // SUBLANES
    n = d.s5_lanes
    weights = [lw["g_mix"], lw["w_in"], lw["g_q"], lw["w_q"], lw["g_kv"], lw["w_ka"], lw["w_uvt"],
               lw["lam"], lw["bblk"], lw["cblk"], lw["s5_d"], lw["glu_w"], lw["glu_b"],
               lw["g_gm_v"], lw["gm_ws"], lw["gm_bias"]]
    out_shape = (
        jax.ShapeDtypeStruct((t, d.qk_cols), BF16),
        jax.ShapeDtypeStruct((t, d.qk_cols), BF16),
        jax.ShapeDtypeStruct((d.heads * d.v_slab, t), BF16),
        jax.ShapeDtypeStruct((t, d.kv_lora), F32),
        jax.ShapeDtypeStruct((t, d.rope), F32),
        jax.ShapeDtypeStruct((t, d.s5_width), BF16),
        jax.ShapeDtypeStruct((t, d.gm_width), BF16),
        jax.ShapeDtypeStruct((1, n), F32),
        jax.ShapeDtypeStruct((1, n), F32),
    )
    out_specs = (
        _row_spec(rows, d.qk_cols), _row_spec(rows, d.qk_cols),
        pl.BlockSpec((d.heads * d.v_slab, rows), lambda i: (0, i)),
        _row_spec(rows, d.kv_lora), _row_spec(rows, d.rope),
        _row_spec(rows, d.s5_width), _row_spec(rows, d.gm_width),
        _full_spec((1, n)), _full_spec((1, n)),
    )
    return pl.pallas_call(
        functools.partial(_proj_prompt_body, d, rows),
        grid=(t // rows,),
        in_specs=[_row_spec(rows, dm), _row_spec(rows, LANES), _row_spec(rows, LANES)]
                 + [_full_spec(w.shape) for w in weights],
        out_specs=out_specs,
        out_shape=out_shape,
        scratch_shapes=[
            pltpu.VMEM((n // LANES, rows, LANES), F32), pltpu.VMEM((n // LANES, rows, LANES), F32),
            pltpu.VMEM((groups, LANES), F32), pltpu.VMEM((groups, LANES), F32),
            pltpu.VMEM((groups, LANES), F32), pltpu.VMEM((groups, LANES), F32),
            pltpu.VMEM((2, n), F32),
        ],
        compiler_params=_params("arbitrary"),
        name="proj_mixers_prompt",
    )(h, ca, sb, *weights)


def _proj_sample(d, h, ca, sb, h0_re, h0_im, lw):
    db, dm = h.shape
    n = d.s5_lanes
    weights = [lw["g_mix"], lw["w_in"], lw["g_q"], lw["w_q"], lw["g_kv"], lw["w_ukp"],
               lw["lam"], lw["bblk"], lw["cblk"], lw["s5_d"], lw["glu_w"], lw["glu_b"],
               lw["g_gm_v"], lw["gm_w0"], lw["gm_b0"]]
    out_shape = (
        jax.ShapeDtypeStruct((db, d.qk_cols), BF16),
        jax.ShapeDtypeStruct((db, d.heads * d.kv_lora), BF16),
        jax.ShapeDtypeStruct((db, d.kv_lora), F32),
        jax.ShapeDtypeStruct((db, d.rope), F32),
        jax.ShapeDtypeStruct((db, d.s5_width), BF16),
        jax.ShapeDtypeStruct((db, d.gm_width), BF16),
        jax.ShapeDtypeStruct((db, n), F32),
        jax.ShapeDtypeStruct((db, n), F32),
        jax.ShapeDtypeStruct((db, d.gm_width), F32),
    )
    args = [h, ca, sb, *weights, h0_re, h0_im]
    return pl.pallas_call(
        functools.partial(_proj_sample_body, d),
        grid=(1,),
        in_specs=[_full_spec(a.shape) for a in args],
        out_specs=tuple(_full_spec(o.shape) for o in out_shape),
        out_shape=out_shape,
        compiler_params=_params("arbitrary"),
        name="proj_mixers_sample",
    )(*args)


def _rope_tables(d, positions):
    inv = ROPE_THETA ** (-jnp.arange(0, d.rope, 2, dtype=F32) / d.rope)
    ang = positions.astype(F32)[:, None] * inv[None, :]
    cos, sin = jnp.cos(ang), jnp.sin(ang)
    t = positions.shape[0]
    ca = jnp.concatenate([cos, cos, jnp.ones((t, d.nope), F32),
                          jnp.zeros((t, LANES - d.rope - d.nope), F32)], axis=1)
    sb = jnp.concatenate([sin, sin, jnp.zeros((t, LANES - d.rope), F32)], axis=1)
    return ca, sb


def _layer_weights(d, l, w, lam, bblk):
    f = {}
    r2 = d.rope // 2
    w_in = w["w_in"][l]
    o = np.cumsum([0, d.q_lora, d.kv_lora, d.rope, d.s5_width, d.gm_width, d.gm_width])
    c_q, c_kv, k_rope, u_s5, u_gm, v_gm = [w_in[:, o[i]:o[i + 1]] for i in range(6)]
    zpad = jnp.zeros((d.d_model, LANES - d.rope), F32)
    k_rot = jnp.concatenate([-k_rope[:, r2:], k_rope[:, :r2]], axis=1)
    f["w_in"] = jnp.concatenate([c_q, c_kv, u_s5, u_gm, v_gm, k_rope, zpad, k_rot, zpad], axis=1).astype(BF16)

    w_uq = w["w_uq"][l].reshape(d.q_lora, d.heads, d.nope + d.rope)
    qn, qr = w_uq[:, :, :d.nope], w_uq[:, :, d.nope:]
    qrot = jnp.concatenate([-qr[:, :, r2:], qr[:, :, :r2]], axis=2)
    hpad = jnp.zeros((d.q_lora, d.heads, LANES - d.rope - d.nope), F32)
    plain = jnp.concatenate([qr, qn, hpad], axis=2).reshape(d.q_lora, d.qk_cols)
    partner = jnp.concatenate([qrot, jnp.zeros_like(qn), hpad], axis=2).reshape(d.q_lora, d.qk_cols)
    f["w_q"] = jnp.concatenate([plain, partner], axis=1).astype(BF16)

    w_uk = w["w_uk"][l]
    kz0 = jnp.zeros((d.kv_lora, d.heads, d.rope), F32)
    kz1 = jnp.zeros((d.kv_lora, d.heads, LANES - d.rope - d.nope), F32)
    f["w_ka"] = jnp.concatenate([kz0, w_uk, kz1], axis=2).reshape(d.kv_lora, d.qk_cols).astype(BF16)
    f["w_ukp"] = jnp.concatenate([kz0, w_uk, kz1], axis=2).transpose(1, 2, 0).astype(BF16)
    w_uv = w["w_uv"][l]
    f["w_uvt"] = w_uv.reshape(d.kv_lora, d.mla_width).T.astype(BF16)
    eye_h = jnp.eye(d.heads, dtype=F32)
    f["w_uv_big"] = (w_uv.transpose(1, 0, 2)[:, :, None, :] * eye_h[:, None, :, None]).reshape(
        d.heads * d.kv_lora, d.mla_width).astype(BF16)

    f["lam"] = lam[l]
    f["bblk"] = bblk[l]
    eye_g = jnp.eye(d.s5_groups, dtype=F32)

    def cblock(cc):
        return (cc.transpose(0, 2, 1)[:, :, None, :] * eye_g[:, None, :, None]).reshape(d.s5_lanes, d.s5_width)

    f["cblk"] = jnp.concatenate([cblock(w["s5_c_re"][l]), -cblock(w["s5_c_im"][l])], axis=0).astype(BF16)
    f["s5_d"] = w["s5_d"][l][None, :]
    f["glu_w"] = w["s5_glu_w"][l].astype(BF16)
    f["glu_b"] = w["s5_glu_b"][l][None, :]

    f["g_gm_v"] = w["g_gm_v"][l][None, :]
    f["gm_ws"] = w["gm_ws"][l]
    gm_bs = w["gm_bs"][l]
    f["gm_bias"] = jnp.repeat(gm_bs.T, d.gm_head_dim, axis=1)
    f["gm_w0"] = jnp.repeat(w["gm_ws"][l][:, 0, 0], d.gm_head_dim)[None, :]
    f["gm_b0"] = jnp.repeat(gm_bs[:, 0], d.gm_head_dim)[None, :]

    for name in ("g_mix", "g_q", "g_kv", "g_ffn", "g_ple"):
        f[name] = w[name][l][None, :]
    for name in ("w_out", "w_up", "w_down", "w_ple", "w_pg"):
        f[name] = w[name][l].astype(BF16)
    return f


def kernel(x_prompt, x_sample, cache_ckv, cache_krope, state_s5_re, state_s5_im, page_table, p_prompt, p_sample, g_mix, w_in, g_q, w_uq, g_kv, w_uk, w_uv, s5_a_re, s5_a_im, s5_log_dt, s5_b_re, s5_b_im, s5_c_re, s5_c_im, s5_d, s5_glu_w, s5_glu_b, g_gm_v, gm_ws, gm_bs, w_out, g_ffn, w_up, w_down, g_ple, w_ple, w_pg, g_final):
    w = dict(g_mix=g_mix, w_in=w_in, g_q=g_q, w_uq=w_uq, g_kv=g_kv, w_uk=w_uk, w_uv=w_uv,
             s5_c_re=s5_c_re, s5_c_im=s5_c_im, s5_d=s5_d, s5_glu_w=s5_glu_w, s5_glu_b=s5_glu_b,
             g_gm_v=g_gm_v, gm_ws=gm_ws, gm_bs=gm_bs, w_out=w_out, g_ffn=g_ffn, w_up=w_up,
             w_down=w_down, g_ple=g_ple, w_ple=w_ple, w_pg=w_pg)
    d = _Dims(w_in, w_uq, w_uk, w_uv, s5_b_re, gm_ws)
    depth = w_in.shape[0]
    batch, seq, dm = x_prompt.shape
    db, dec_seq, _ = x_sample.shape
    assert batch == 1 and dec_seq == 1
    past_len = page_table.shape[1] * cache_ckv.shape[2]

    lam, bblk = _s5_discretise(s5_a_re, s5_a_im, s5_log_dt, s5_b_re, s5_b_im)
    ca_p, sb_p = _rope_tables(d, jnp.arange(seq))
    ca_s, sb_s = _rope_tables(d, jnp.full((db,), past_len))
    g_fin = g_final[None, :]
    cache_krope_t = jnp.swapaxes(cache_krope, 2, 3)

    hp = x_prompt.reshape(seq, dm)
    hs = x_sample.reshape(db, dm)
    outs = [[] for _ in range(9)]
    for l in range(depth):
        lw = _layer_weights(d, l, w, lam, bblk)
        last = l == depth - 1
        q, k, vt, ckv_p, kr_p, os5, ogm, sre_p, sim_p = _proj_prompt(d, hp, ca_p, sb_p, lw)
        att = _flash_attention(d, q, k, vt)
        hp = _out_layer(d, hp, att, os5, ogm, p_prompt[l].reshape(seq, -1), None, lw, g_fin if last else None)
        qs, qlat, ckv_s, kr_s, os5_s, ogm_s, sre_s, sim_s, vn_s = _proj_sample(
            d, hs, ca_s, sb_s, state_s5_re[l].reshape(db, -1), state_s5_im[l].reshape(db, -1), lw)
        olat = _paged_attention(d, l, page_table, qlat, qs, ckv_s, kr_s, cache_ckv, cache_krope_t)
        hs = _out_layer(d, hs, olat.reshape(db, -1), os5_s, ogm_s, p_sample[l].reshape(db, -1),
                        lw["w_uv_big"], lw, g_fin if last else None)
        gshape = (d.s5_groups, d.s5_state)
        for lst, val in zip(outs, (
                ckv_p.reshape(batch, seq, -1), kr_p.reshape(batch, seq, -1),
                ckv_s.reshape(db, dec_seq, -1), kr_s.reshape(db, dec_seq, -1),
                sre_p.reshape(batch, *gshape), sim_p.reshape(batch, *gshape),
                sre_s.reshape(db, *gshape), sim_s.reshape(db, *gshape),
                vn_s.reshape(db, dec_seq, -1))):
            lst.append(val)
    return (hp.reshape(batch, seq, dm), hs.reshape(db, dec_seq, dm), *[jnp.stack(o) for o in outs])
```

```python
import functools
import math

import numpy as np
import jax
import jax.numpy as jnp
from jax import lax
from jax.experimental import pallas as pl
from jax.experimental.pallas import tpu as pltpu

F32 = jnp.float32
BF16 = jnp.bfloat16

EPS = 1e-6
ROPE_THETA = 10000.0
NEG_INIT = -1e30
PAGE_SIZE = 128

LANES = 128
SUBLANES = 8
BF16_SUBLANES = 16
VMEM_LIMIT_BYTES = 56 * 1024 * 1024

PROJ_ROWS = 256
OUT_ROWS = 512
FLASH_KEY_BLOCK = 512
FLASH_QUERY_GROUPS = 4
FFN_CHUNK = 1024
PAGES_PER_STEP = 16


def _dot(a, b):
    return jnp.dot(a, b, preferred_element_type=F32)


def _dot_nt(a, b):
    return lax.dot_general(a, b, (((1,), (1,)), ((), ())), preferred_element_type=F32)


def _rms(x, g):
    return x * lax.rsqrt(jnp.mean(x * x, axis=-1, keepdims=True) + EPS) * g


def _full_spec(shape):
    nd = len(shape)
    return pl.BlockSpec(shape, lambda *_: (0,) * nd)


def _row_spec(rows, cols):
    return pl.BlockSpec((rows, cols), lambda i, *_: (i, 0))


def _params(*sem):
    return pltpu.CompilerParams(dimension_semantics=sem, vmem_limit_bytes=VMEM_LIMIT_BYTES)


def _s5_discretise_body(are_ref, aim_ref, ldt_ref, bre_ref, bim_ref, abre_ref, abim_ref, bbre_ref, bbim_ref):
    a_re = are_ref[...]
    a_im = aim_ref[...]
    dt = jnp.exp(ldt_ref[...])
    mag = jnp.exp(a_re * dt)
    ab_re = mag * jnp.cos(a_im * dt)
    ab_im = mag * jnp.sin(a_im * dt)
    den = a_re * a_re + a_im * a_im
    n_re = ab_re - 1.0
    n_im = ab_im
    k_re = (n_re * a_re + n_im * a_im) / den
    k_im = (n_im * a_re - n_re * a_im) / den
    b_re = bre_ref[...]
    b_im = bim_ref[...]
    abre_ref[...] = ab_re
    abim_ref[...] = ab_im
    bbre_ref[...] = k_re * b_re - k_im * b_im
    bbim_ref[...] = k_re * b_im + k_im * b_re


def _s5_discretise(a_re, a_im, log_dt, b_re, b_im):
    L, G, P, CH = b_re.shape
    rows = L * G * CH
    rep = lambda x: jnp.broadcast_to(x[:, :, None, :], (L, G, CH, P)).reshape(rows, P)
    a_re_r, a_im_r = rep(a_re), rep(a_im)
    ldt_r = jnp.broadcast_to(log_dt[:, :, None, None], (L, G, CH, P)).reshape(rows, P)
    bt_re = b_re.transpose(0, 1, 3, 2).reshape(rows, P)
    bt_im = b_im.transpose(0, 1, 3, 2).reshape(rows, P)
    out = jax.ShapeDtypeStruct((rows, P), F32)
    ab_re, ab_im, bb_re, bb_im = pl.pallas_call(
        _s5_discretise_body, out_shape=(out, out, out, out), name="s5_discretise")(
            a_re_r, a_im_r, ldt_r, bt_re, bt_im)
    lam = jnp.stack([ab_re.reshape(L, G, CH, P)[:, :, 0, :].reshape(L, G * P),
                     ab_im.reshape(L, G, CH, P)[:, :, 0, :].reshape(L, G * P)], axis=1)
    eye = jnp.eye(G, dtype=F32)

    def blockdiag(bb):
        bb = bb.reshape(L, G, CH, P)
        return (bb[:, :, :, None, :] * eye[None, :, None, :, None]).reshape(L, G * CH, G * P)

    bblk = jnp.concatenate([blockdiag(bb_re), blockdiag(bb_im)], axis=-1).astype(BF16)
    return lam, bblk


class _Dims:
    def __init__(self, w_in, w_uq, w_uk, w_uv, s5_b_re, gm_ws):
        self.d_model = w_in.shape[1]
        self.kv_lora = w_uk.shape[1]
        self.heads = w_uk.shape[2]
        self.nope = w_uk.shape[3]
        self.v_dim = w_uv.shape[3]
        self.q_lora = w_uq.shape[1]
        self.rope = w_uq.shape[2] // self.heads - self.nope
        self.s5_groups, self.s5_state, self.s5_ch = s5_b_re.shape[1:]
        self.s5_width = self.s5_groups * self.s5_ch
        self.s5_lanes = self.s5_groups * self.s5_state
        self.gm_heads = gm_ws.shape[1]
        self.chunk = gm_ws.shape[2]
        self.gm_width = (w_in.shape[2] - self.q_lora - self.kv_lora - self.rope - self.s5_width) // 2
        self.gm_head_dim = self.gm_width // self.gm_heads
        self.mla_width = self.heads * self.v_dim
        self.v_slab = self.v_dim + BF16_SUBLANES
        self.attn_scale = 1.0 / math.sqrt(self.nope + self.rope)
        assert self.rope + self.nope <= LANES and self.rope % 2 == 0
        self.qk_cols = self.heads * LANES
        self.o_cq = 0
        self.o_ckv = self.o_cq + self.q_lora
        self.o_s5 = self.o_ckv + self.kv_lora
        self.o_ugm = self.o_s5 + self.s5_width
        self.o_vgm = self.o_ugm + self.gm_width
        self.o_kra = self.o_vgm + self.gm_width
        self.o_krb = self.o_kra + LANES
        self.in_cols = self.o_krb + LANES
        for off in (self.o_ckv, self.o_s5, self.o_ugm, self.o_vgm, self.o_kra):
            assert off % LANES == 0


def _project_common(d, q_scale, h_ref, ca_ref, sb_ref, gmix_ref, win_ref, gq_ref, wq_ref, gkv_ref,
                    q_ref, ckv_ref, kr_ref):
    a = _rms(h_ref[...], gmix_ref[...]).astype(BF16)
    z = _dot(a, win_ref[...])
    ca = ca_ref[...]
    sb = sb_ref[...]
    cqn = _rms(z[:, d.o_cq:d.o_cq + d.q_lora], gq_ref[...]).astype(BF16)
    qq = _dot(cqn, wq_ref[...])
    for hh in range(d.heads):
        lo = hh * LANES
        q_ref[:, lo:lo + LANES] = (
            (qq[:, lo:lo + LANES] * ca + qq[:, d.qk_cols + lo:d.qk_cols + lo + LANES] * sb)
            * q_scale).astype(BF16)
    ckv = _rms(z[:, d.o_ckv:d.o_ckv + d.kv_lora], gkv_ref[...])
    ckv_ref[...] = ckv
    kr_slot = z[:, d.o_kra:d.o_kra + LANES] * ca + z[:, d.o_krb:d.o_krb + LANES] * sb
    kr_ref[...] = kr_slot[:, :d.rope]
    u_s5 = z[:, d.o_s5:d.o_s5 + d.s5_width]
    u_gm = z[:, d.o_ugm:d.o_ugm + d.gm_width]
    v_gm = z[:, d.o_vgm:d.o_vgm + d.gm_width]
    return q_ref, ckv, kr_slot, u_s5, u_gm, v_gm


def _s5_readout(d, hr_b, hi_b, u_s5, cblk_ref, dskip_ref, gluw_ref, glub_ref):
    n = d.s5_lanes
    y = _dot(hr_b, cblk_ref[0:n, :]) + _dot(hi_b, cblk_ref[n:2 * n, :]) + dskip_ref[...] * u_s5
    g = jax.nn.gelu(y)
    zz = _dot(g.astype(BF16), gluw_ref[...]) + glub_ref[...]
    w = d.s5_width
    return zz[:, :w] * jax.nn.sigmoid(zz[:, w:])


def _cmul_add(ar, ai, br, bi, cr, ci):
    return ar * br - ai * bi + cr, ar * bi + ai * br + ci


def _proj_prompt_body(d, rows,
                      h_ref, ca_ref, sb_ref, gmix_ref, win_ref, gq_ref, wq_ref, gkv_ref, wka_ref, wuvt_ref,
                      lam_ref, bblk_ref, cblk_ref, dskip_ref, gluw_ref, glub_ref, ggm_ref, gws_ref, gbias_ref,
                      q_ref, k_ref, vt_ref, ckv_ref, kr_ref, os5_ref, ogm_ref, sre_ref, sim_ref,
                      xr_scr, xi_scr, er_scr, ei_scr, cr_scr, ci_scr, st_scr):
    @pl.when(pl.program_id(0) == 0)
    def _():
        st_scr[...] = jnp.zeros_like(st_scr)

    _, ckv, kr_slot, u_s5, u_gm, v_gm = _project_common(
        d, d.attn_scale * math.log2(math.e),
        h_ref, ca_ref, sb_ref, gmix_ref, win_ref, gq_ref, wq_ref, gkv_ref, q_ref, ckv_ref, kr_ref)

    ckvb = ckv.astype(BF16)
    kk = _dot(ckvb, wka_ref[...])
    for hh in range(d.heads):
        lo = hh * LANES
        k_ref[:, lo:lo + LANES] = (kk[:, lo:lo + LANES] + kr_slot).astype(BF16)
    vt = _dot_nt(wuvt_ref[...], ckvb).astype(BF16)
    ones_tile = (lax.broadcasted_iota(jnp.int32, (BF16_SUBLANES, rows), 0) == 0).astype(BF16)
    for hh in range(d.heads):
        vlo = hh * d.v_slab
        vt_ref[vlo:vlo + d.v_dim, :] = vt[hh * d.v_dim:(hh + 1) * d.v_dim, :]
        vt_ref[vlo + d.v_dim:vlo + d.v_slab, :] = ones_tile

    n = d.s5_lanes
    bu = _dot(u_s5.astype(BF16), bblk_ref[...])
    groups = rows // SUBLANES
    for s in range(n // LANES):
        lo = s * LANES
        xr = xr_scr.at[s]
        xi = xi_scr.at[s]
        xr[...] = bu[:, lo:lo + LANES]
        xi[...] = bu[:, n + lo:n + lo + LANES]
        lam_r = lam_ref[0:1, lo:lo + LANES]
        lam_i = lam_ref[1:2, lo:lo + LANES]
        pw_r, pw_i = [lam_r], [lam_i]
        for _ in range(SUBLANES - 1):
            nr, ni = _cmul_add(pw_r[-1], pw_i[-1], lam_r, lam_i, 0.0, 0.0)
            pw_r.append(nr)
            pw_i.append(ni)
        hr = xr[pl.ds(0, groups, stride=SUBLANES), :]
        hi = xi[pl.ds(0, groups, stride=SUBLANES), :]
        for r in range(1, SUBLANES):
            hr, hi = _cmul_add(lam_r, lam_i, hr, hi,
                               xr[pl.ds(r, groups, stride=SUBLANES), :],
                               xi[pl.ds(r, groups, stride=SUBLANES), :])
            xr[pl.ds(r, groups, stride=SUBLANES), :] = hr
            xi[pl.ds(r, groups, stride=SUBLANES), :] = hi
        er_scr[...] = hr
        ei_scr[...] = hi
        cr = st_scr[0:1, lo:lo + LANES]
        ci = st_scr[1:2, lo:lo + LANES]
        for j in range(groups):
            cr_scr[j:j + 1, :] = cr
            ci_scr[j:j + 1, :] = ci
            cr, ci = _cmul_add(pw_r[-1], pw_i[-1], cr, ci, er_scr[j:j + 1, :], ei_scr[j:j + 1, :])
        st_scr[0:1, lo:lo + LANES] = cr
        st_scr[1:2, lo:lo + LANES] = ci
        cin_r = cr_scr[...]
        cin_i = ci_scr[...]
        for r in range(SUBLANES):
            hr, hi = _cmul_add(pw_r[r], pw_i[r], cin_r, cin_i,
                               xr[pl.ds(r, groups, stride=SUBLANES), :],
                               xi[pl.ds(r, groups, stride=SUBLANES), :])
            xr[pl.ds(r, groups, stride=SUBLANES), :] = hr
            xi[pl.ds(r, groups, stride=SUBLANES), :] = hi
    sre_ref[...] = st_scr[0:1, :]
    sim_ref[...] = st_scr[1:2, :]
    h_re = jnp.concatenate([xr_scr[s] for s in range(n // LANES)], axis=1).astype(BF16)
    h_im = jnp.concatenate([xi_scr[s] for s in range(n // LANES)], axis=1).astype(BF16)
    o_s5 = _s5_readout(d, h_re, h_im, u_s5,
                       cblk_ref, dskip_ref, gluw_ref, glub_ref)
    os5_ref[...] = o_s5.astype(BF16)

    vn = _rms(v_gm, ggm_ref[...])
    c = d.chunk
    tril = lax.broadcasted_iota(jnp.int32, (c, c), 1) <= lax.broadcasted_iota(jnp.int32, (c, c), 0)
    head_of_lane = lax.broadcasted_iota(jnp.int32, (c, d.gm_width), 1) // d.gm_head_dim
    w_low = [jnp.where(tril, gws_ref[g], 0.0).astype(BF16) for g in range(d.gm_heads)]
    for cc in range(rows // c):
        vc = vn[cc * c:(cc + 1) * c, :]
        s = gbias_ref[...]
        for g in range(d.gm_heads):
            s = s + _dot(w_low[g], jnp.where(head_of_lane == g, vc, 0.0).astype(BF16))
        ogm_ref[cc * c:(cc + 1) * c, :] = (u_gm[cc * c:(cc + 1) * c, :] * s).astype(BF16)


def _proj_sample_body(d,
                      h_ref, ca_ref, sb_ref, gmix_ref, win_ref, gq_ref, wq_ref, gkv_ref, wukp_ref,
                      lam_ref, bblk_ref, cblk_ref, dskip_ref, gluw_ref, glub_ref, ggm_ref, gw0_ref, gb0_ref,
                      h0r_ref, h0i_ref,
                      q_ref, qlat_ref, ckv_ref, kr_ref, os5_ref, ogm_ref, sre_ref, sim_ref, vn_ref):
    _, _, _, u_s5, u_gm, v_gm = _project_common(
        d, d.attn_scale,
        h_ref, ca_ref, sb_ref, gmix_ref, win_ref, gq_ref, wq_ref, gkv_ref, q_ref, ckv_ref, kr_ref)

    for hh in range(d.heads):
        qlat_ref[:, hh * d.kv_lora:(hh + 1) * d.kv_lora] = _dot(
            q_ref[:, hh * LANES:(hh + 1) * LANES], wukp_ref[hh]).astype(BF16)

    n = d.s5_lanes
    bu = _dot(u_s5.astype(BF16), bblk_ref[...])
    hr, hi = _cmul_add(lam_ref[0:1, :], lam_ref[1:2, :], h0r_ref[...], h0i_ref[...], bu[:, :n], bu[:, n:])
    sre_ref[...] = hr
    sim_ref[...] = hi
    o_s5 = _s5_readout(d, hr.astype(BF16), hi.astype(BF16), u_s5, cblk_ref, dskip_ref, gluw_ref, glub_ref)
    os5_ref[...] = o_s5.astype(BF16)

    vn = _rms(v_gm, ggm_ref[...])
    vn_ref[...] = vn
    ogm_ref[...] = (u_gm * (gw0_ref[...] * vn + gb0_ref[...])).astype(BF16)


def _flash_body(d, kb, groups, qi_ref, ki_ref, q_ref, k_ref, vt_ref, o_ref, m_scr, acc_scr, s_scr):
    step = pl.program_id(0)
    ki = ki_ref[step]
    rel = ki - qi_ref[step] * groups

    @pl.when(ki == 0)
    def _():
        m_scr[...] = jnp.full_like(m_scr, NEG_INIT)
        acc_scr[...] = jnp.zeros_like(acc_scr)

    def scores(unit, slot):
        hh, g = unit
        lo = hh * LANES
        s_scr[slot] = _dot_nt(k_ref[:, lo:lo + LANES], q_ref[g * kb:(g + 1) * kb, lo:lo + LANES])

    def update(diag_group):
        first = 0 if diag_group is None else diag_group
        units = [(hh, g) for hh in range(d.heads) for g in range(first, groups)]
        scores(units[0], 0)
        for u, (hh, g) in enumerate(units):
            if u + 1 < len(units):
                scores(units[u + 1], (u + 1) % 2)
            st = s_scr[u % 2]
            if g == diag_group:
                keep = (lax.broadcasted_iota(jnp.int32, (kb, kb), 0)
                        <= lax.broadcasted_iota(jnp.int32, (kb, kb), 1))
                st = jnp.where(keep, st, NEG_INIT)
            cols = slice(g * kb, (g + 1) * kb)
            m_prev = m_scr[hh:hh + 1, cols]
            m_new = jnp.maximum(m_prev, jnp.max(st, axis=0, keepdims=True))
            alpha = jnp.exp2(m_prev - m_new)
            p = jnp.exp2(st - m_new).astype(BF16)
            vlo = hh * d.v_slab
            acc_scr[vlo:vlo + d.v_slab, cols] = alpha * acc_scr[vlo:vlo + d.v_slab, cols] + _dot(
                vt_ref[vlo:vlo + d.v_slab, :], p)
            m_scr[hh:hh + 1, cols] = m_new

    @pl.when(rel < 0)
    def _():
        update(None)

    for dg in range(groups):
        @pl.when(rel == dg)
        def _():
            update(dg)

    @pl.when(rel == groups - 1)
    def _():
        outs = []
        for hh in range(d.heads):
            vlo = hh * d.v_slab
            outs.append(acc_scr[vlo:vlo + d.v_dim, :] / acc_scr[vlo + d.v_dim:vlo + d.v_dim + 1, :])
        o_ref[...] = jnp.concatenate(outs, axis=0).T.astype(BF16)


def _flash_attention(d, q, k, vt):
    t = q.shape[0]
    kb = min(FLASH_KEY_BLOCK, t)
    groups = min(FLASH_QUERY_GROUPS, t // kb)
    qb = kb * groups
    assert t % qb == 0
    nq = t // qb
    qi = np.concatenate([np.full((i + 1) * groups, i, np.int32) for i in range(nq)])
    ki = np.concatenate([np.arange((i + 1) * groups, dtype=np.int32) for i in range(nq)])
    grid_spec = pltpu.PrefetchScalarGridSpec(
        num_scalar_prefetch=2,
        grid=(len(qi),),
        in_specs=[
            pl.BlockSpec((qb, d.qk_cols), lambda s, qi, ki: (qi[s], 0)),
            pl.BlockSpec((kb, d.qk_cols), lambda s, qi, ki: (ki[s], 0)),
            pl.BlockSpec((d.heads * d.v_slab, kb), lambda s, qi, ki: (0, ki[s])),
        ],
        out_specs=pl.BlockSpec((qb, d.mla_width), lambda s, qi, ki: (qi[s], 0)),
        scratch_shapes=[
            pltpu.VMEM((d.heads, qb), F32),
            pltpu.VMEM((d.heads * d.v_slab, qb), F32),
            pltpu.VMEM((2, kb, kb), F32),
        ],
    )
    return pl.pallas_call(
        functools.partial(_flash_body, d, kb, groups),
        grid_spec=grid_spec,
        out_shape=jax.ShapeDtypeStruct((t, d.mla_width), BF16),
        compiler_params=_params("arbitrary"),
        name="prompt_flash_attention",
    )(jnp.asarray(qi), jnp.asarray(ki), q, k, vt)


def _paged_body(d, layer, pages, n_chunks, pt_ref, qlat_ref, q_ref, ckvn_ref, krn_ref, kc_hbm, kp_hbm,
                o_ref, kc_buf, kp_buf, sem, kcb_scr, p_scr, corr_scr, m_scr, l_scr, acc_scr):
    b = pl.program_id(0)
    rows = pl.num_programs(0)

    def chunk_copies(row, c, slot):
        cps = []
        for j in range(pages):
            pg = pt_ref[row, c * pages + j]
            cps.append(pltpu.make_async_copy(kc_hbm.at[layer, pg], kc_buf.at[slot, j], sem.at[0, slot]))
            cps.append(pltpu.make_async_copy(kp_hbm.at[layer, pg], kp_buf.at[slot, j], sem.at[1, slot]))
        return cps

    @pl.when(b == 0)
    def _():
        for cp in chunk_copies(0, 0, 0):
            cp.start()

        kcb_scr[1] = jnp.zeros(kcb_scr.shape[1:], BF16)

    ql = qlat_ref[...]
    qr = q_ref[:, 0:d.rope]
    m_scr[...] = jnp.full_like(m_scr, NEG_INIT)
    l_scr[...] = jnp.zeros_like(l_scr)
    acc_scr[...] = jnp.zeros_like(acc_scr)
    p_scr[...] = jnp.zeros_like(p_scr)
    corr_scr[...] = jnp.ones_like(corr_scr)
    tile_pages = 2
    tile_keys = tile_pages * PAGE_SIZE
    n_tiles = pages // tile_pages

    def value_matmuls(kslot):
        pb = p_scr[...]
        pv = None
        for t in range(n_tiles):
            part = _dot(pb[:, t * tile_keys:(t + 1) * tile_keys],
                        kcb_scr[kslot, t * tile_keys:(t + 1) * tile_keys, :])
            pv = part if pv is None else pv + part
        acc_scr[...] = acc_scr[...] * corr_scr[...] + pv

    def chunk(c, carry):
        slot = (b * n_chunks + c) % 2
        last_of_row = c == n_chunks - 1
        nxt_row = jnp.where(last_of_row, b + 1, b)
        nxt_c = jnp.where(last_of_row, 0, c + 1)

        @pl.when(nxt_row < rows)
        def _():
            for cp in chunk_copies(nxt_row, nxt_c, 1 - slot):
                cp.start()

        for cp in chunk_copies(b, c, slot):
            cp.wait()

        kslot = c % 2
        s_tiles = []
        for t in range(n_tiles):
            js = range(t * tile_pages, (t + 1) * tile_pages)
            kcb = jnp.concatenate([kc_buf[slot, j].astype(BF16) for j in js], axis=0)
            kpb = jnp.concatenate([kp_buf[slot, j].astype(BF16) for j in js], axis=1)
            kcb_scr[kslot, t * tile_keys:(t + 1) * tile_keys, :] = kcb
            s_tiles.append(_dot_nt(ql, kcb) + _dot(qr, kpb))
        value_matmuls(1 - kslot)
        s = jnp.concatenate(s_tiles, axis=1)
        m_prev = m_scr[...]
        m_new = jnp.maximum(m_prev, jnp.max(s, axis=-1, keepdims=True))
        corr = jnp.exp(m_prev - m_new)
        p = jnp.exp(s - m_new)
        l_scr[...] = l_scr[...] * corr + jnp.sum(p, axis=-1, keepdims=True)
        p_scr[...] = p.astype(BF16)
        corr_scr[...] = corr
        m_scr[...] = m_new
        return carry

    lax.fori_loop(0, n_chunks, chunk, 0)
    value_matmuls((n_chunks - 1) % 2)

    cn = ckvn_ref[...]
    s_self = (jnp.sum(ql.astype(F32) * cn, axis=-1, keepdims=True)
              + jnp.sum(qr.astype(F32) * krn_ref[...], axis=-1, keepdims=True))
    m_prev = m_scr[...]
    m_new = jnp.maximum(m_prev, s_self)
    corr = jnp.exp(m_prev - m_new)
    p = jnp.exp(s_self - m_new)
    l = l_scr[...] * corr + p
    acc = acc_scr[...] * corr + p * cn
    o_ref[...] = (acc / l).astype(BF16)


def _paged_attention(d, layer, page_table, q_lat, q, ckv_new, kr_new, cache_ckv, cache_krope_t):
    db, n_pages = page_table.shape
    pages = min(PAGES_PER_STEP, n_pages)
    assert n_pages % pages == 0 and cache_ckv.shape[2] == PAGE_SIZE

    assert pages % 2 == 0 and (n_pages // pages) % 2 == 0
    grid_spec = pltpu.PrefetchScalarGridSpec(
        num_scalar_prefetch=1,
        grid=(db,),
        in_specs=[
            pl.BlockSpec((None, d.heads, d.kv_lora), lambda b, pt: (b, 0, 0)),
            pl.BlockSpec((None, d.heads, LANES), lambda b, pt: (b, 0, 0)),
            pl.BlockSpec((None, 1, d.kv_lora), lambda b, pt: (b, 0, 0)),
            pl.BlockSpec((None, 1, d.rope), lambda b, pt: (b, 0, 0)),
            pl.BlockSpec(memory_space=pl.ANY),
            pl.BlockSpec(memory_space=pl.ANY),
        ],
        out_specs=pl.BlockSpec((None, d.heads, d.kv_lora), lambda b, pt: (b, 0, 0)),
        scratch_shapes=[
            pltpu.VMEM((2, pages, PAGE_SIZE, d.kv_lora), F32),
            pltpu.VMEM((2, pages, d.rope, PAGE_SIZE), F32),
            pltpu.SemaphoreType.DMA((2, 2)),
            pltpu.VMEM((2, pages * PAGE_SIZE, d.kv_lora), BF16),
            pltpu.VMEM((d.heads, pages * PAGE_SIZE), BF16),
            pltpu.VMEM((d.heads, 1), F32),
            pltpu.VMEM((d.heads, 1), F32),
            pltpu.VMEM((d.heads, 1), F32),
            pltpu.VMEM((d.heads, d.kv_lora), F32),
        ],
    )
    return pl.pallas_call(
        functools.partial(_paged_body, d, layer, pages, n_pages // pages),
        grid_spec=grid_spec,
        out_shape=jax.ShapeDtypeStruct((db, d.heads, d.kv_lora), BF16),
        compiler_params=_params("arbitrary"),
        name="decode_paged_attention",
    )(page_table,
      q_lat.reshape(db, d.heads, d.kv_lora), q.reshape(db, d.heads, LANES),
      ckv_new.reshape(db, 1, d.kv_lora), kr_new.reshape(db, 1, d.rope),
      cache_ckv, cache_krope_t)


def _out_body(d, latent_attn, final_norm, d_ff, *refs):
    refs = list(refs)
    h_ref, att_ref, os5_ref, ogm_ref, ple_ref = refs[:5]
    refs = refs[5:]
    if latent_attn:
        wuvb_ref = refs.pop(0)
    wout_ref, gffn_ref, wup_ref, wdown_ref, gple_ref, wple_ref, wpg_ref = refs[:7]
    refs = refs[7:]
    if final_norm:
        gfin_ref = refs.pop(0)
    out_ref, = refs

    if latent_attn:
        o_mla = _dot(att_ref[...], wuvb_ref[...]).astype(BF16)
    else:
        o_mla = att_ref[...]
    m0 = d.mla_width
    m1 = m0 + d.s5_width
    m2 = m1 + d.gm_width
    h = (h_ref[...] + _dot(o_mla, wout_ref[0:m0, :]) + _dot(os5_ref[...], wout_ref[m0:m1, :])
         + _dot(ogm_ref[...], wout_ref[m1:m2, :]))
    a = _rms(h, gffn_ref[...]).astype(BF16)
    fc = min(FFN_CHUNK, d_ff)
    ffn = None
    for cc in range(d_ff // fc):
        f = jnp.maximum(_dot(a, wup_ref[:, cc * fc:(cc + 1) * fc]), 0.0)
        part = _dot((f * f).astype(BF16), wdown_ref[cc * fc:(cc + 1) * fc, :])
        ffn = part if ffn is None else ffn + part
    h = h + ffn
    gate = jax.nn.sigmoid(_dot(_rms(h, gple_ref[...]).astype(BF16), wpg_ref[...]))
    h = h + _dot(ple_ref[...].astype(BF16), wple_ref[...]) * gate
    if final_norm:
        h = _rms(h, gfin_ref[...])
    out_ref[...] = h


def _out_layer(d, h, att, o_s5, o_gm, ple, wuv_big, lw, g_final):
    t, dm = h.shape
    rows = min(OUT_ROWS, t)
    assert t % rows == 0
    d_ff = lw["w_up"].shape[1]
    latent_attn = wuv_big is not None
    final_norm = g_final is not None
    def wspec(x):
        return pl.BlockSpec(x.shape, lambda i: (0,) * x.ndim, pipeline_mode=pl.Buffered(1))

    args = [h, att, o_s5, o_gm, ple]
    specs = [_row_spec(rows, dm), _row_spec(rows, att.shape[1]), _row_spec(rows, o_s5.shape[1]),
             _row_spec(rows, o_gm.shape[1]), _row_spec(rows, ple.shape[1])]
    weights = ([wuv_big] if latent_attn else []) + [
        lw["w_out"], lw["g_ffn"], lw["w_up"], lw["w_down"], lw["g_ple"], lw["w_ple"], lw["w_pg"]]
    if final_norm:
        weights.append(g_final)
    return pl.pallas_call(
        functools.partial(_out_body, d, latent_attn, final_norm, d_ff),
        grid=(t // rows,),
        in_specs=specs + [wspec(w) for w in weights],
        out_specs=_row_spec(rows, dm),
        out_shape=jax.ShapeDtypeStruct((t, dm), F32),
        compiler_params=_params("parallel"),
        name="out_ffn_ple",
    )(*args, *weights)


def _proj_prompt(d, h, ca, sb, lw):
    t, dm = h.shape
    rows = min(PROJ_ROWS, t)
    assert t % rows == 0 and rows % d.chunk == 0 and rows % SUBLANES == 0
    groups = rows // SUBLANES
    n = d.s5_lanes
    weights = [lw["g_mix"], lw["w_in"], lw["g_q"], lw["w_q"], lw["g_kv"], lw["w_ka"], lw["w_uvt"],
               lw["lam"], lw["bblk"], lw["cblk"], lw["s5_d"], lw["glu_w"], lw["glu_b"],
               lw["g_gm_v"], lw["gm_ws"], lw["gm_bias"]]
    out_shape = (
        jax.ShapeDtypeStruct((t, d.qk_cols), BF16),
        jax.ShapeDtypeStruct((t, d.qk_cols), BF16),
        jax.ShapeDtypeStruct((d.heads * d.v_slab, t), BF16),
        jax.ShapeDtypeStruct((t, d.kv_lora), F32),
        jax.ShapeDtypeStruct((t, d.rope), F32),
        jax.ShapeDtypeStruct((t, d.s5_width), BF16),
        jax.ShapeDtypeStruct((t, d.gm_width), BF16),
        jax.ShapeDtypeStruct((1, n), F32),
        jax.ShapeDtypeStruct((1, n), F32),
    )
    out_specs = (
        _row_spec(rows, d.qk_cols), _row_spec(rows, d.qk_cols),
        pl.BlockSpec((d.heads * d.v_slab, rows), lambda i: (0, i)),
        _row_spec(rows, d.kv_lora), _row_spec(rows, d.rope),
        _row_spec(rows, d.s5_width), _row_spec(rows, d.gm_width),
        _full_spec((1, n)), _full_spec((1, n)),
    )
    return pl.pallas_call(
        functools.partial(_proj_prompt_body, d, rows),
        grid=(t // rows,),
        in_specs=[_row_spec(rows, dm), _row_spec(rows, LANES), _row_spec(rows, LANES)]
                 + [_full_spec(w.shape) for w in weights],
        out_specs=out_specs,
        out_shape=out_shape,
        scratch_shapes=[
            pltpu.VMEM((n // LANES, rows, LANES), F32), pltpu.VMEM((n // LANES, rows, LANES), F32),
            pltpu.VMEM((groups, LANES), F32), pltpu.VMEM((groups, LANES), F32),
            pltpu.VMEM((groups, LANES), F32), pltpu.VMEM((groups, LANES), F32),
            pltpu.VMEM((2, n), F32),
        ],
        compiler_params=_params("arbitrary"),
        name="proj_mixers_prompt",
    )(h, ca, sb, *weights)


def _proj_sample(d, h, ca, sb, h0_re, h0_im, lw):
    db, dm = h.shape
    n = d.s5_lanes
    weights = [lw["g_mix"], lw["w_in"], lw["g_q"], lw["w_q"], lw["g_kv"], lw["w_ukp"],
               lw["lam"], lw["bblk"], lw["cblk"], lw["s5_d"], lw["glu_w"], lw["glu_b"],
               lw["g_gm_v"], lw["gm_w0"], lw["gm_b0"]]
    out_shape = (
        jax.ShapeDtypeStruct((db, d.qk_cols), BF16),
        jax.ShapeDtypeStruct((db, d.heads * d.kv_lora), BF16),
        jax.ShapeDtypeStruct((db, d.kv_lora), F32),
        jax.ShapeDtypeStruct((db, d.rope), F32),
        jax.ShapeDtypeStruct((db, d.s5_width), BF16),
        jax.ShapeDtypeStruct((db, d.gm_width), BF16),
        jax.ShapeDtypeStruct((db, n), F32),
        jax.ShapeDtypeStruct((db, n), F32),
        jax.ShapeDtypeStruct((db, d.gm_width), F32),
    )
    args = [h, ca, sb, *weights, h0_re, h0_im]
    return pl.pallas_call(
        functools.partial(_proj_sample_body, d),
        grid=(1,),
        in_specs=[_full_spec(a.shape) for a in args],
        out_specs=tuple(_full_spec(o.shape) for o in out_shape),
        out_shape=out_shape,
        compiler_params=_params("arbitrary"),
        name="proj_mixers_sample",
    )(*args)


def _rope_tables(d, positions):
    inv = ROPE_THETA ** (-jnp.arange(0, d.rope, 2, dtype=F32) / d.rope)
    ang = positions.astype(F32)[:, None] * inv[None, :]
    cos, sin = jnp.cos(ang), jnp.sin(ang)
    t = positions.shape[0]
    ca = jnp.concatenate([cos, cos, jnp.ones((t, d.nope), F32),
                          jnp.zeros((t, LANES - d.rope - d.nope), F32)], axis=1)
    sb = jnp.concatenate([sin, sin, jnp.zeros((t, LANES - d.rope), F32)], axis=1)
    return ca, sb


def _layer_weights(d, l, w, lam, bblk):
    f = {}
    r2 = d.rope // 2
    w_in = w["w_in"][l]
    o = np.cumsum([0, d.q_lora, d.kv_lora, d.rope, d.s5_width, d.gm_width, d.gm_width])
    c_q, c_kv, k_rope, u_s5, u_gm, v_gm = [w_in[:, o[i]:o[i + 1]] for i in range(6)]
    zpad = jnp.zeros((d.d_model, LANES - d.rope), F32)
    k_rot = jnp.concatenate([-k_rope[:, r2:], k_rope[:, :r2]], axis=1)
    f["w_in"] = jnp.concatenate([c_q, c_kv, u_s5, u_gm, v_gm, k_rope, zpad, k_rot, zpad], axis=1).astype(BF16)

    w_uq = w["w_uq"][l].reshape(d.q_lora, d.heads, d.nope + d.rope)
    qn, qr = w_uq[:, :, :d.nope], w_uq[:, :, d.nope:]
    qrot = jnp.concatenate([-qr[:, :, r2:], qr[:, :, :r2]], axis=2)
    hpad = jnp.zeros((d.q_lora, d.heads, LANES - d.rope - d.nope), F32)
    plain = jnp.concatenate([qr, qn, hpad], axis=2).reshape(d.q_lora, d.qk_cols)
    partner = jnp.concatenate([qrot, jnp.zeros_like(qn), hpad], axis=2).reshape(d.q_lora, d.qk_cols)
    f["w_q"] = jnp.concatenate([plain, partner], axis=1).astype(BF16)

    w_uk = w["w_uk"][l]
    kz0 = jnp.zeros((d.kv_lora, d.heads, d.rope), F32)
    kz1 = jnp.zeros((d.kv_lora, d.heads, LANES - d.rope - d.nope), F32)
    f["w_ka"] = jnp.concatenate([kz0, w_uk, kz1], axis=2).reshape(d.kv_lora, d.qk_cols).astype(BF16)
    f["w_ukp"] = jnp.concatenate([kz0, w_uk, kz1], axis=2).transpose(1, 2, 0).astype(BF16)
    w_uv = w["w_uv"][l]
    f["w_uvt"] = w_uv.reshape(d.kv_lora, d.mla_width).T.astype(BF16)
    eye_h = jnp.eye(d.heads, dtype=F32)
    f["w_uv_big"] = (w_uv.transpose(1, 0, 2)[:, :, None, :] * eye_h[:, None, :, None]).reshape(
        d.heads * d.kv_lora, d.mla_width).astype(BF16)

    f["lam"] = lam[l]
    f["bblk"] = bblk[l]
    eye_g = jnp.eye(d.s5_groups, dtype=F32)

    def cblock(cc):
        return (cc.transpose(0, 2, 1)[:, :, None, :] * eye_g[:, None, :, None]).reshape(d.s5_lanes, d.s5_width)

    f["cblk"] = jnp.concatenate([cblock(w["s5_c_re"][l]), -cblock(w["s5_c_im"][l])], axis=0).astype(BF16)
    f["s5_d"] = w["s5_d"][l][None, :]
    f["glu_w"] = w["s5_glu_w"][l].astype(BF16)
    f["glu_b"] = w["s5_glu_b"][l][None, :]

    f["g_gm_v"] = w["g_gm_v"][l][None, :]
    f["gm_ws"] = w["gm_ws"][l]
    gm_bs = w["gm_bs"][l]
    f["gm_bias"] = jnp.repeat(gm_bs.T, d.gm_head_dim, axis=1)
    f["gm_w0"] = jnp.repeat(w["gm_ws"][l][:, 0, 0], d.gm_head_dim)[None, :]
    f["gm_b0"] = jnp.repeat(gm_bs[:, 0], d.gm_head_dim)[None, :]

    for name in ("g_mix", "g_q", "g_kv", "g_ffn", "g_ple"):
        f[name] = w[name][l][None, :]
    for name in ("w_out", "w_up", "w_down", "w_ple", "w_pg"):
        f[name] = w[name][l].astype(BF16)
    return f


def kernel(x_prompt, x_sample, cache_ckv, cache_krope, state_s5_re, state_s5_im, page_table, p_prompt, p_sample, g_mix, w_in, g_q, w_uq, g_kv, w_uk, w_uv, s5_a_re, s5_a_im, s5_log_dt, s5_b_re, s5_b_im, s5_c_re, s5_c_im, s5_d, s5_glu_w, s5_glu_b, g_gm_v, gm_ws, gm_bs, w_out, g_ffn, w_up, w_down, g_ple, w_ple, w_pg, g_final):
    w = dict(g_mix=g_mix, w_in=w_in, g_q=g_q, w_uq=w_uq, g_kv=g_kv, w_uk=w_uk, w_uv=w_uv,
             s5_c_re=s5_c_re, s5_c_im=s5_c_im, s5_d=s5_d, s5_glu_w=s5_glu_w, s5_glu_b=s5_glu_b,
             g_gm_v=g_gm_v, gm_ws=gm_ws, gm_bs=gm_bs, w_out=w_out, g_ffn=g_ffn, w_up=w_up,
             w_down=w_down, g_ple=g_ple, w_ple=w_ple, w_pg=w_pg)
    d = _Dims(w_in, w_uq, w_uk, w_uv, s5_b_re, gm_ws)
    depth = w_in.shape[0]
    batch, seq, dm = x_prompt.shape
    db, dec_seq, _ = x_sample.shape
    assert batch == 1 and dec_seq == 1
    past_len = page_table.shape[1] * cache_ckv.shape[2]

    lam, bblk = _s5_discretise(s5_a_re, s5_a_im, s5_log_dt, s5_b_re, s5_b_im)
    ca_p, sb_p = _rope_tables(d, jnp.arange(seq))
    ca_s, sb_s = _rope_tables(d, jnp.full((db,), past_len))
    g_fin = g_final[None, :]
    cache_krope_t = jnp.swapaxes(cache_krope, 2, 3)

    hp = x_prompt.reshape(seq, dm)
    hs = x_sample.reshape(db, dm)
    outs = [[] for _ in range(9)]
    for l in range(depth):
        lw = _layer_weights(d, l, w, lam, bblk)
        last = l == depth - 1
        q, k, vt, ckv_p, kr_p, os5, ogm, sre_p, sim_p = _proj_prompt(d, hp, ca_p, sb_p, lw)
        att = _flash_attention(d, q, k, vt)
        hp = _out_layer(d, hp, att, os5, ogm, p_prompt[l].reshape(seq, -1), None, lw, g_fin if last else None)
        qs, qlat, ckv_s, kr_s, os5_s, ogm_s, sre_s, sim_s, vn_s = _proj_sample(
            d, hs, ca_s, sb_s, state_s5_re[l].reshape(db, -1), state_s5_im[l].reshape(db, -1), lw)
        olat = _paged_attention(d, l, page_table, qlat, qs, ckv_s, kr_s, cache_ckv, cache_krope_t)
        hs = _out_layer(d, hs, olat.reshape(db, -1), os5_s, ogm_s, p_sample[l].reshape(db, -1),
                        lw["w_uv_big"], lw, g_fin if last else None)
        gshape = (d.s5_groups, d.s5_state)
        for lst, val in zip(outs, (
                ckv_p.reshape(batch, seq, -1), kr_p.reshape(batch, seq, -1),
                ckv_s.reshape(db, dec_seq, -1), kr_s.reshape(db, dec_seq, -1),
                sre_p.reshape(batch, *gshape), sim_p.reshape(batch, *gshape),
                sre_s.reshape(db, *gshape), sim_s.reshape(db, *gshape),
                vn_s.reshape(db, dec_seq, -1))):
            lst.append(val)
    return (hp.reshape(batch, seq, dm), hs.reshape(db, dec_seq, dm), *[jnp.stack(o) for o in outs])
```

```python
import functools
import math

import numpy as np
import jax
import jax.numpy as jnp
from jax import lax
from jax.experimental import pallas as pl
from jax.experimental.pallas import tpu as pltpu

F32 = jnp.float32
BF16 = jnp.bfloat16

EPS = 1e-6
ROPE_THETA = 10000.0
NEG_INIT = -1e30
PAGE_SIZE = 128

LANES = 128
SUBLANES = 8
BF16_SUBLANES = 16
VMEM_LIMIT_BYTES = 56 * 1024 * 1024

PROJ_ROWS = 256
OUT_ROWS = 512
FLASH_KEY_BLOCK = 512
FLASH_QUERY_GROUPS = 4
FFN_CHUNK = 1024
PAGES_PER_STEP = 32
PAGED_SLOTS = 4


def _dot(a, b):
    return jnp.dot(a, b, preferred_element_type=F32)


def _dot_nt(a, b):
    return lax.dot_general(a, b, (((1,), (1,)), ((), ())), preferred_element_type=F32)


def _rms(x, g):
    return x * lax.rsqrt(jnp.mean(x * x, axis=-1, keepdims=True) + EPS) * g


def _full_spec(shape):
    nd = len(shape)
    return pl.BlockSpec(shape, lambda *_: (0,) * nd)


def _row_spec(rows, cols):
    return pl.BlockSpec((rows, cols), lambda i, *_: (i, 0))


def _params(*sem):
    return pltpu.CompilerParams(dimension_semantics=sem, vmem_limit_bytes=VMEM_LIMIT_BYTES)


def _s5_discretise_body(are_ref, aim_ref, ldt_ref, bre_ref, bim_ref, abre_ref, abim_ref, bbre_ref, bbim_ref):
    a_re = are_ref[...]
    a_im = aim_ref[...]
    dt = jnp.exp(ldt_ref[...])
    mag = jnp.exp(a_re * dt)
    ab_re = mag * jnp.cos(a_im * dt)
    ab_im = mag * jnp.sin(a_im * dt)
    den = a_re * a_re + a_im * a_im
    n_re = ab_re - 1.0
    n_im = ab_im
    k_re = (n_re * a_re + n_im * a_im) / den
    k_im = (n_im * a_re - n_re * a_im) / den
    b_re = bre_ref[...]
    b_im = bim_ref[...]
    abre_ref[...] = ab_re
    abim_ref[...] = ab_im
    bbre_ref[...] = k_re * b_re - k_im * b_im
    bbim_ref[...] = k_re * b_im + k_im * b_re


def _s5_discretise(a_re, a_im, log_dt, b_re, b_im):
    L, G, P, CH = b_re.shape
    rows = L * G * CH
    rep = lambda x: jnp.broadcast_to(x[:, :, None, :], (L, G, CH, P)).reshape(rows, P)
    a_re_r, a_im_r = rep(a_re), rep(a_im)
    ldt_r = jnp.broadcast_to(log_dt[:, :, None, None], (L, G, CH, P)).reshape(rows, P)
    bt_re = b_re.transpose(0, 1, 3, 2).reshape(rows, P)
    bt_im = b_im.transpose(0, 1, 3, 2).reshape(rows, P)
    out = jax.ShapeDtypeStruct((rows, P), F32)
    ab_re, ab_im, bb_re, bb_im = pl.pallas_call(
        _s5_discretise_body, out_shape=(out, out, out, out), name="s5_discretise")(
            a_re_r, a_im_r, ldt_r, bt_re, bt_im)
    lam = jnp.stack([ab_re.reshape(L, G, CH, P)[:, :, 0, :].reshape(L, G * P),
                     ab_im.reshape(L, G, CH, P)[:, :, 0, :].reshape(L, G * P)], axis=1)
    eye = jnp.eye(G, dtype=F32)

    def blockdiag(bb):
        bb = bb.reshape(L, G, CH, P)
        return (bb[:, :, :, None, :] * eye[None, :, None, :, None]).reshape(L, G * CH, G * P)

    bblk = jnp.concatenate([blockdiag(bb_re), blockdiag(bb_im)], axis=-1).astype(BF16)
    return lam, bblk


class _Dims:
    def __init__(self, w_in, w_uq, w_uk, w_uv, s5_b_re, gm_ws):
        self.d_model = w_in.shape[1]
        self.kv_lora = w_uk.shape[1]
        self.heads = w_uk.shape[2]
        self.nope = w_uk.shape[3]
        self.v_dim = w_uv.shape[3]
        self.q_lora = w_uq.shape[1]
        self.rope = w_uq.shape[2] // self.heads - self.nope
        self.s5_groups, self.s5_state, self.s5_ch = s5_b_re.shape[1:]
        self.s5_width = self.s5_groups * self.s5_ch
        self.s5_lanes = self.s5_groups * self.s5_state
        self.gm_heads = gm_ws.shape[1]
        self.chunk = gm_ws.shape[2]
        self.gm_width = (w_in.shape[2] - self.q_lora - self.kv_lora - self.rope - self.s5_width) // 2
        self.gm_head_dim = self.gm_width // self.gm_heads
        self.mla_width = self.heads * self.v_dim
        self.v_slab = self.v_dim + BF16_SUBLANES
        self.attn_scale = 1.0 / math.sqrt(self.nope + self.rope)
        assert self.rope + self.nope <= LANES and self.rope % 2 == 0
        self.qk_cols = self.heads * LANES
        self.o_cq = 0
        self.o_ckv = self.o_cq + self.q_lora
        self.o_s5 = self.o_ckv + self.kv_lora
        self.o_ugm = self.o_s5 + self.s5_width
        self.o_vgm = self.o_ugm + self.gm_width
        self.o_kra = self.o_vgm + self.gm_width
        self.o_krb = self.o_kra + LANES
        self.in_cols = self.o_krb + LANES
        for off in (self.o_ckv, self.o_s5, self.o_ugm, self.o_vgm, self.o_kra):
            assert off % LANES == 0


def _project_common(d, q_scale, h_ref, ca_ref, sb_ref, gmix_ref, win_ref, gq_ref, wq_ref, gkv_ref,
                    q_ref, ckv_ref, kr_ref, after_in_proj=None):
    a = _rms(h_ref[...], gmix_ref[...]).astype(BF16)
    z = _dot(a, win_ref[...])
    if after_in_proj is not None:
        after_in_proj(z)
    ca = ca_ref[...]
    sb = sb_ref[...]
    cqn = _rms(z[:, d.o_cq:d.o_cq + d.q_lora], gq_ref[...]).astype(BF16)
    qq = _dot(cqn, wq_ref[...])
    for hh in range(d.heads):
        lo = hh * LANES
        q_ref[:, lo:lo + LANES] = (
            (qq[:, lo:lo + LANES] * ca + qq[:, d.qk_cols + lo:d.qk_cols + lo + LANES] * sb)
            * q_scale).astype(BF16)
    ckv = _rms(z[:, d.o_ckv:d.o_ckv + d.kv_lora], gkv_ref[...])
    ckv_ref[...] = ckv
    kr_slot = z[:, d.o_kra:d.o_kra + LANES] * ca + z[:, d.o_krb:d.o_krb + LANES] * sb
    kr_ref[...] = kr_slot[:, :d.rope]
    u_s5 = z[:, d.o_s5:d.o_s5 + d.s5_width]
    u_gm = z[:, d.o_ugm:d.o_ugm + d.gm_width]
    v_gm = z[:, d.o_vgm:d.o_vgm + d.gm_width]
    return q_ref, ckv, kr_slot, u_s5, u_gm, v_gm


def _s5_readout(d, hr_b, hi_b, u_s5, cblk_ref, dskip_ref, gluw_ref, glub_ref):
    n = d.s5_lanes
    y = _dot(hr_b, cblk_ref[0:n, :]) + _dot(hi_b, cblk_ref[n:2 * n, :]) + dskip_ref[...] * u_s5
    g = jax.nn.gelu(y)
    zz = _dot(g.astype(BF16), gluw_ref[...]) + glub_ref[...]
    w = d.s5_width
    return zz[:, :w] * jax.nn.sigmoid(zz[:, w:])


def _cmul_add(ar, ai, br, bi, cr, ci):
    return ar * br - ai * bi + cr, ar * bi + ai * br + ci


def _proj_prompt_body(d, rows,
                      h_ref, ca_ref, sb_ref, gmix_ref, win_ref, gq_ref, wq_ref, gkv_ref, wka_ref, wuvt_ref,
                      lam_ref, bblk_ref, cblk_ref, dskip_ref, gluw_ref, glub_ref, ggm_ref, gws_ref, gbias_ref,
                      q_ref, k_ref, vt_ref, ckv_ref, kr_ref, os5_ref, ogm_ref, sre_ref, sim_ref,
                      xr_scr, xi_scr, er_scr, ei_scr, cr_scr, ci_scr, st_scr):
    @pl.when(pl.program_id(0) == 0)
    def _():
        st_scr[...] = jnp.zeros_like(st_scr)

    n = d.s5_lanes
    n_slabs = n // LANES

    def s5_input(z):
        bu = _dot(z[:, d.o_s5:d.o_s5 + d.s5_width].astype(BF16), bblk_ref[...])
        for s in range(n_slabs):
            xr_scr[s] = bu[:, s * LANES:(s + 1) * LANES]
            xi_scr[s] = bu[:, n + s * LANES:n + (s + 1) * LANES]

    _, ckv, kr_slot, u_s5, u_gm, v_gm = _project_common(
        d, d.attn_scale * math.log2(math.e),
        h_ref, ca_ref, sb_ref, gmix_ref, win_ref, gq_ref, wq_ref, gkv_ref, q_ref, ckv_ref, kr_ref,
        after_in_proj=s5_input)

    ckvb = ckv.astype(BF16)
    kk = _dot(ckvb, wka_ref[...])
    for hh in range(d.heads):
        lo = hh * LANES
        k_ref[:, lo:lo + LANES] = (kk[:, lo:lo + LANES] + kr_slot).astype(BF16)
    vt = _dot_nt(wuvt_ref[...], ckvb).astype(BF16)
    ones_tile = (lax.broadcasted_iota(jnp.int32, (BF16_SUBLANES, rows), 0) == 0).astype(BF16)
    for hh in range(d.heads):
        vlo = hh * d.v_slab
        vt_ref[vlo:vlo + d.v_dim, :] = vt[hh * d.v_dim:(hh + 1) * d.v_dim, :]
        vt_ref[vlo + d.v_dim:vlo + d.v_slab, :] = ones_tile

    vn = _rms(v_gm, ggm_ref[...])
    c = d.chunk
    tril = lax.broadcasted_iota(jnp.int32, (c, c), 1) <= lax.broadcasted_iota(jnp.int32, (c, c), 0)
    head_of_lane = lax.broadcasted_iota(jnp.int32, (c, d.gm_width), 1) // d.gm_head_dim
    w_low = [jnp.where(tril, gws_ref[g], 0.0).astype(BF16) for g in range(d.gm_heads)]
    for cc in range(rows // c):
        vc = vn[cc * c:(cc + 1) * c, :]
        s = gbias_ref[...]
        for g in range(d.gm_heads):
            s = s + _dot(w_low[g], jnp.where(head_of_lane == g, vc, 0.0).astype(BF16))
        ogm_ref[cc * c:(cc + 1) * c, :] = (u_gm[cc * c:(cc + 1) * c, :] * s).astype(BF16)

    groups = rows // SUBLANES
    for s in range(n_slabs):
        lo = s * LANES
        xr = xr_scr.at[s]
        xi = xi_scr.at[s]
        lam_r = lam_ref[0:1, lo:lo + LANES]
        lam_i = lam_ref[1:2, lo:lo + LANES]
        pw_r, pw_i = [lam_r], [lam_i]
        for _ in range(SUBLANES - 1):
            nr, ni = _cmul_add(pw_r[-1], pw_i[-1], lam_r, lam_i, 0.0, 0.0)
            pw_r.append(nr)
            pw_i.append(ni)
        hr = xr[pl.ds(0, groups, stride=SUBLANES), :]
        hi = xi[pl.ds(0, groups, stride=SUBLANES), :]
        for r in range(1, SUBLANES):
            hr, hi = _cmul_add(lam_r, lam_i, hr, hi,
                               xr[pl.ds(r, groups, stride=SUBLANES), :],
                               xi[pl.ds(r, groups, stride=SUBLANES), :])
            xr[pl.ds(r, groups, stride=SUBLANES), :] = hr
            xi[pl.ds(r, groups, stride=SUBLANES), :] = hi
        er_scr[...] = hr
        ei_scr[...] = hi
        cr = st_scr[0:1, lo:lo + LANES]
        ci = st_scr[1:2, lo:lo + LANES]
        for j in range(groups):
            cr_scr[j:j + 1, :] = cr
            ci_scr[j:j + 1, :] = ci
            cr, ci = _cmul_add(pw_r[-1], pw_i[-1], cr, ci, er_scr[j:j + 1, :], ei_scr[j:j + 1, :])
        st_scr[0:1, lo:lo + LANES] = cr
        st_scr[1:2, lo:lo + LANES] = ci
        cin_r = cr_scr[...]
        cin_i = ci_scr[...]
        for r in range(SUBLANES):
            hr, hi = _cmul_add(pw_r[r], pw_i[r], cin_r, cin_i,
                               xr[pl.ds(r, groups, stride=SUBLANES), :],
                               xi[pl.ds(r, groups, stride=SUBLANES), :])
            xr[pl.ds(r, groups, stride=SUBLANES), :] = hr
            xi[pl.ds(r, groups, stride=SUBLANES), :] = hi
    sre_ref[...] = st_scr[0:1, :]
    sim_ref[...] = st_scr[1:2, :]
    h_re = jnp.concatenate([xr_scr[s] for s in range(n // LANES)], axis=1).astype(BF16)
    h_im = jnp.concatenate([xi_scr[s] for s in range(n // LANES)], axis=1).astype(BF16)
    o_s5 = _s5_readout(d, h_re, h_im, u_s5,
                       cblk_ref, dskip_ref, gluw_ref, glub_ref)
    os5_ref[...] = o_s5.astype(BF16)


def _proj_sample_body(d,
                      h_ref, ca_ref, sb_ref, gmix_ref, win_ref, gq_ref, wq_ref, gkv_ref, wukp_ref,
                      lam_ref, bblk_ref, cblk_ref, dskip_ref, gluw_ref, glub_ref, ggm_ref, gw0_ref, gb0_ref,
                      h0r_ref, h0i_ref,
                      q_ref, qlat_ref, ckv_ref, kr_ref, os5_ref, ogm_ref, sre_ref, sim_ref, vn_ref):
    _, _, _, u_s5, u_gm, v_gm = _project_common(
        d, d.attn_scale,
        h_ref, ca_ref, sb_ref, gmix_ref, win_ref, gq_ref, wq_ref, gkv_ref, q_ref, ckv_ref, kr_ref)

    for hh in range(d.heads):
        qlat_ref[:, hh * d.kv_lora:(hh + 1) * d.kv_lora] = _dot(
            q_ref[:, hh * LANES:(hh + 1) * LANES], wukp_ref[hh]).astype(BF16)

    n = d.s5_lanes
    bu = _dot(u_s5.astype(BF16), bblk_ref[...])
    hr, hi = _cmul_add(lam_ref[0:1, :], lam_ref[1:2, :], h0r_ref[...], h0i_ref[...], bu[:, :n], bu[:, n:])
    sre_ref[...] = hr
    sim_ref[...] = hi
    o_s5 = _s5_readout(d, hr.astype(BF16), hi.astype(BF16), u_s5, cblk_ref, dskip_ref, gluw_ref, glub_ref)
    os5_ref[...] = o_s5.astype(BF16)

    vn = _rms(v_gm, ggm_ref[...])
    vn_ref[...] = vn
    ogm_ref[...] = (u_gm * (gw0_ref[...] * vn + gb0_ref[...])).astype(BF16)


def _flash_body(d, kb, groups, qi_ref, ki_ref, q_ref, k_ref, vt_ref, o_ref, m_scr, acc_scr, s_scr):
    step = pl.program_id(0)
    ki = ki_ref[step]
    rel = ki - qi_ref[step] * groups

    @pl.when(ki == 0)
    def _():
        m_scr[...] = jnp.full_like(m_scr, NEG_INIT)
        acc_scr[...] = jnp.zeros_like(acc_scr)

    def scores(unit, slot):
        hh, g = unit
        lo = hh * LANES
        s_scr[slot] = _dot_nt(k_ref[:, lo:lo + LANES], q_ref[g * kb:(g + 1) * kb, lo:lo + LANES])

    def update(diag_group):
        first = 0 if diag_group is None else diag_group
        units = [(hh, g) for hh in range(d.heads) for g in range(first, groups)]
        scores(units[0], 0)
        for u, (hh, g) in enumerate(units):
            if u + 1 < len(units):
                scores(units[u + 1], (u + 1) % 2)
            st = s_scr[u % 2]
            if g == diag_group:
                keep = (lax.broadcasted_iota(jnp.int32, (kb, kb), 0)
                        <= lax.broadcasted_iota(jnp.int32, (kb, kb), 1))
                st = jnp.where(keep, st, NEG_INIT)
            cols = slice(g * kb, (g + 1) * kb)
            m_prev = m_scr[hh:hh + 1, cols]
            m_new = jnp.maximum(m_prev, jnp.max(st, axis=0, keepdims=True))
            alpha = jnp.exp2(m_prev - m_new)
            p = jnp.exp2(st - m_new).astype(BF16)
            vlo = hh * d.v_slab
            acc_scr[vlo:vlo + d.v_slab, cols] = alpha * acc_scr[vlo:vlo + d.v_slab, cols] + _dot(
                vt_ref[vlo:vlo + d.v_slab, :], p)
            m_scr[hh:hh + 1, cols] = m_new

    @pl.when(rel < 0)
    def _():
        update(None)

    for dg in range(groups):
        @pl.when(rel == dg)
        def _():
            update(dg)

    @pl.when(rel == groups - 1)
    def _():
        outs = []
        for hh in range(d.heads):
            vlo = hh * d.v_slab
            outs.append(acc_scr[vlo:vlo + d.v_dim, :] / acc_scr[vlo + d.v_dim:vlo + d.v_dim + 1, :])
        o_ref[...] = jnp.concatenate(outs, axis=0).T.astype(BF16)


def _flash_attention(d, q, k, vt):
    t = q.shape[0]
    kb = min(FLASH_KEY_BLOCK, t)
    groups = min(FLASH_QUERY_GROUPS, t // kb)
    qb = kb * groups
    assert t % qb == 0
    nq = t // qb
    qi = np.concatenate([np.full((i + 1) * groups, i, np.int32) for i in range(nq)])
    ki = np.concatenate([np.arange((i + 1) * groups, dtype=np.int32) for i in range(nq)])
    grid_spec = pltpu.PrefetchScalarGridSpec(
        num_scalar_prefetch=2,
        grid=(len(qi),),
        in_specs=[
            pl.BlockSpec((qb, d.qk_cols), lambda s, qi, ki: (qi[s], 0)),
            pl.BlockSpec((kb, d.qk_cols), lambda s, qi, ki: (ki[s], 0)),
            pl.BlockSpec((d.heads * d.v_slab, kb), lambda s, qi, ki: (0, ki[s])),
        ],
        out_specs=pl.BlockSpec((qb, d.mla_width), lambda s, qi, ki: (qi[s], 0)),
        scratch_shapes=[
            pltpu.VMEM((d.heads, qb), F32),
            pltpu.VMEM((d.heads * d.v_slab, qb), F32),
            pltpu.VMEM((2, kb, kb), F32),
        ],
    )
    return pl.pallas_call(
        functools.partial(_flash_body, d, kb, groups),
        grid_spec=grid_spec,
        out_shape=jax.ShapeDtypeStruct((t, d.mla_width), BF16),
        compiler_params=_params("arbitrary"),
        name="prompt_flash_attention",
    )(jnp.asarray(qi), jnp.asarray(ki), q, k, vt)


def _paged_body(d, layer, pages, n_chunks, n_rows, pt_ref, qlat_ref, q_ref, ckvn_ref, krn_ref, kc_hbm, kp_hbm,
                o_ref, kc_buf, kp_buf, sem, kcb_scr, p_scr, corr_scr, m_scr, l_scr, acc_scr):
    b = pl.program_id(0)
    rows = pl.num_programs(0)

    def chunk_copies(row, c, slot, page_range=range(pages)):
        cps = []
        for j in page_range:
            pg = pt_ref[row, c * pages + j]
            cps.append(pltpu.make_async_copy(kc_hbm.at[layer, pg], kc_buf.at[slot, j], sem.at[0, slot]))
            cps.append(pltpu.make_async_copy(kp_hbm.at[layer, pg], kp_buf.at[slot, j], sem.at[1, slot]))
        return cps

    ahead = PAGED_SLOTS - 1
    assert ahead <= n_chunks

    @pl.when(b == 0)
    def _():
        for g in range(ahead):
            for cp in chunk_copies(0, g, g % PAGED_SLOTS):
                cp.start()

        kcb_scr[1] = jnp.zeros(kcb_scr.shape[1:], BF16)

    ql = qlat_ref[...]
    qr = q_ref[:, 0:d.rope]
    m_scr[...] = jnp.full_like(m_scr, NEG_INIT)
    l_scr[...] = jnp.zeros_like(l_scr)
    acc_scr[...] = jnp.zeros_like(acc_scr)
    p_scr[...] = jnp.zeros_like(p_scr)
    corr_scr[...] = jnp.ones_like(corr_scr)
    tile_pages = 2
    tile_keys = tile_pages * PAGE_SIZE
    n_tiles = pages // tile_pages

    def value_matmuls(kslot):
        pb = p_scr[...]
        pv = None
        for t in range(n_tiles):
            part = _dot(pb[:, t * tile_keys:(t + 1) * tile_keys],
                        kcb_scr[kslot, t * tile_keys:(t + 1) * tile_keys, :])
            pv = part if pv is None else pv + part
        acc_scr[...] = acc_scr[...] * corr_scr[...] + pv

    def chunk(c, carry):
        g = b * n_chunks + c
        slot = lax.rem(g, PAGED_SLOTS)
        wraps = c + ahead >= n_chunks
        nxt_row = jnp.minimum(jnp.where(wraps, b + 1, b), rows - 1)
        nxt_c = jnp.where(wraps, c + ahead - n_chunks, c + ahead)
        nxt_slot = lax.rem(g + ahead, PAGED_SLOTS)

        for cp in chunk_copies(b, c, slot):
            cp.wait()

        kslot = c % 2
        s_tiles = []
        for t in range(n_tiles):
            js = range(t * tile_pages, (t + 1) * tile_pages)
            kcb = jnp.concatenate([kc_buf[slot, j].astype(BF16) for j in js], axis=0)
            kpb = jnp.concatenate([kp_buf[slot, j].astype(BF16) for j in js], axis=1)
            kcb_scr[kslot, t * tile_keys:(t + 1) * tile_keys, :] = kcb
            s_tiles.append(_dot_nt(ql, kcb) + _dot(qr, kpb))
            for cp in chunk_copies(nxt_row, nxt_c, nxt_slot, js):
                cp.start()
        value_matmuls(1 - kslot)
        s = jnp.concatenate(s_tiles, axis=1)
        m_prev = m_scr[...]
        m_new = jnp.maximum(m_prev, jnp.max(s, axis=-1, keepdims=True))
        corr = jnp.exp(m_prev - m_new)
        p = jnp.exp(s - m_new)
        l_scr[...] = l_scr[...] * corr + jnp.sum(p, axis=-1, keepdims=True)
        p_scr[...] = p.astype(BF16)
        corr_scr[...] = corr
        m_scr[...] = m_new
        return carry

    lax.fori_loop(0, n_chunks, chunk, 0)
    value_matmuls((n_chunks - 1) % 2)

    @pl.when(b == rows - 1)
    def _():
        for extra in range(ahead):
            for cp in chunk_copies(n_rows - 1, extra, (n_rows * n_chunks + extra) % PAGED_SLOTS):
                cp.wait()

    cn = ckvn_ref[...]
    s_self = (jnp.sum(ql.astype(F32) * cn, axis=-1, keepdims=True)
              + jnp.sum(qr.astype(F32) * krn_ref[...], axis=-1, keepdims=True))
    m_prev = m_scr[...]
    m_new = jnp.maximum(m_prev, s_self)
    corr = jnp.exp(m_prev - m_new)
    p = jnp.exp(s_self - m_new)
    l = l_scr[...] * corr + p
    acc = acc_scr[...] * corr + p * cn
    o_ref[...] = (acc / l).astype(BF16)


def _paged_attention(d, layer, page_table, q_lat, q, ckv_new, kr_new, cache_ckv, cache_krope_t):
    db, n_pages = page_table.shape
    pages = min(PAGES_PER_STEP, n_pages)
    assert n_pages % pages == 0 and cache_ckv.shape[2] == PAGE_SIZE

    assert pages % 2 == 0 and (n_pages // pages) % 2 == 0
    grid_spec = pltpu.PrefetchScalarGridSpec(
        num_scalar_prefetch=1,
        grid=(db,),
        in_specs=[
            pl.BlockSpec((None, d.heads, d.kv_lora), lambda b, pt: (b, 0, 0)),
            pl.BlockSpec((None, d.heads, LANES), lambda b, pt: (b, 0, 0)),
            pl.BlockSpec((None, 1, d.kv_lora), lambda b, pt: (b, 0, 0)),
            pl.BlockSpec((None, 1, d.rope), lambda b, pt: (b, 0, 0)),
            pl.BlockSpec(memory_space=pl.ANY),
            pl.BlockSpec(memory_space=pl.ANY),
        ],
        out_specs=pl.BlockSpec((None, d.heads, d.kv_lora), lambda b, pt: (b, 0, 0)),
        scratch_shapes=[
            pltpu.VMEM((PAGED_SLOTS, pages, PAGE_SIZE, d.kv_lora), F32),
            pltpu.VMEM((PAGED_SLOTS, pages, d.rope, PAGE_SIZE), F32),
            pltpu.SemaphoreType.DMA((2, PAGED_SLOTS)),
            pltpu.VMEM((2, pages * PAGE_SIZE, d.kv_lora), BF16),
            pltpu.VMEM((d.heads, pages * PAGE_SIZE), BF16),
            pltpu.VMEM((d.heads, 1), F32),
            pltpu.VMEM((d.heads, 1), F32),
            pltpu.VMEM((d.heads, 1), F32),
            pltpu.VMEM((d.heads, d.kv_lora), F32),
        ],
    )
    return pl.pallas_call(
        functools.partial(_paged_body, d, layer, pages, n_pages // pages, db),
        grid_spec=grid_spec,
        out_shape=jax.ShapeDtypeStruct((db, d.heads, d.kv_lora), BF16),
        compiler_params=_params("arbitrary"),
        name="decode_paged_attention",
    )(page_table,
      q_lat.reshape(db, d.heads, d.kv_lora), q.reshape(db, d.heads, LANES),
      ckv_new.reshape(db, 1, d.kv_lora), kr_new.reshape(db, 1, d.rope),
      cache_ckv, cache_krope_t)


def _out_body(d, latent_attn, final_norm, d_ff, *refs):
    refs = list(refs)
    h_ref, att_ref, os5_ref, ogm_ref, ple_ref = refs[:5]
    refs = refs[5:]
    if latent_attn:
        wuvb_ref = refs.pop(0)
    wout_ref, gffn_ref, wup_ref, wdown_ref, gple_ref, wple_ref, wpg_ref = refs[:7]
    refs = refs[7:]
    if final_norm:
        gfin_ref = refs.pop(0)
    out_ref, = refs

    if latent_attn:
        o_mla = _dot(att_ref[...], wuvb_ref[...]).astype(BF16)
    else:
        o_mla = att_ref[...]
    m0 = d.mla_width
    m1 = m0 + d.s5_width
    m2 = m1 + d.gm_width
    h = (h_ref[...] + _dot(o_mla, wout_ref[0:m0, :]) + _dot(os5_ref[...], wout_ref[m0:m1, :])
         + _dot(ogm_ref[...], wout_ref[m1:m2, :]))
    a = _rms(h, gffn_ref[...]).astype(BF16)
    fc = min(FFN_CHUNK, d_ff)
    ffn = None
    for cc in range(d_ff // fc):
        f = jnp.maximum(_dot(a, wup_ref[:, cc * fc:(cc + 1) * fc]), 0.0)
        part = _dot((f * f).astype(BF16), wdown_ref[cc * fc:(cc + 1) * fc, :])
        ffn = part if ffn is None else ffn + part
    h = h + ffn
    gate = jax.nn.sigmoid(_dot(_rms(h, gple_ref[...]).astype(BF16), wpg_ref[...]))
    h = h + _dot(ple_ref[...].astype(BF16), wple_ref[...]) * gate
    if final_norm:
        h = _rms(h, gfin_ref[...])
    out_ref[...] = h


def _out_layer(d, h, att, o_s5, o_gm, ple, wuv_big, lw, g_final):
    t, dm = h.shape
    rows = min(OUT_ROWS, t)
    assert t % rows == 0
    d_ff = lw["w_up"].shape[1]
    latent_attn = wuv_big is not None
    final_norm = g_final is not None
    def wspec(x):
        return pl.BlockSpec(x.shape, lambda i: (0,) * x.ndim, pipeline_mode=pl.Buffered(1))

    args = [h, att, o_s5, o_gm, ple]
    specs = [_row_spec(rows, dm), _row_spec(rows, att.shape[1]), _row_spec(rows, o_s5.shape[1]),
             _row_spec(rows, o_gm.shape[1]), _row_spec(rows, ple.shape[1])]
    weights = ([wuv_big] if latent_attn else []) + [
        lw["w_out"], lw["g_ffn"], lw["w_up"], lw["w_down"], lw["g_ple"], lw["w_ple"], lw["w_pg"]]
    if final_norm:
        weights.append(g_final)
    return pl.pallas_call(
        functools.partial(_out_body, d, latent_attn, final_norm, d_ff),
        grid=(t // rows,),
        in_specs=specs + [wspec(w) for w in weights],
        out_specs=_row_spec(rows, dm),
        out_shape=jax.ShapeDtypeStruct((t, dm), F32),
        compiler_params=_params("parallel"),
        name="out_ffn_ple",
    )(*args, *weights)


def _proj_prompt(d, h, ca, sb, lw):
    t, dm = h.shape
    rows = min(PROJ_ROWS, t)
    assert t % rows == 0 and rows % d.chunk == 0 and rows % SUBLANES == 0
    groups = rows // SUBLANES
    n = d.s5_lanes
    weights = [lw["g_mix"], lw["w_in"], lw["g_q"], lw["w_q"], lw["g_kv"], lw["w_ka"], lw["w_uvt"],
               lw["lam"], lw["bblk"], lw["cblk"], lw["s5_d"], lw["glu_w"], lw["glu_b"],
               lw["g_gm_v"], lw["gm_ws"], lw["gm_bias"]]
    out_shape = (
        jax.ShapeDtypeStruct((t, d.qk_cols), BF16),
        jax.ShapeDtypeStruct((t, d.qk_cols), BF16),
        jax.ShapeDtypeStruct((d.heads * d.v_slab, t), BF16),
        jax.ShapeDtypeStruct((t, d.kv_lora), F32),
        jax.ShapeDtypeStruct((t, d.rope), F32),
        jax.ShapeDtypeStruct((t, d.s5_width), BF16),
        jax.ShapeDtypeStruct((t, d.gm_width), BF16),
        jax.ShapeDtypeStruct((1, n), F32),
        jax.ShapeDtypeStruct((1, n), F32),
    )
    out_specs = (
        _row_spec(rows, d.qk_cols), _row_spec(rows, d.qk_cols),
        pl.BlockSpec((d.heads * d.v_slab, rows), lambda i: (0, i)),
        _row_spec(rows, d.kv_lora), _row_spec(rows, d.rope),
        _row_spec(rows, d.s5_width), _row_spec(rows, d.gm_width),
        _full_spec((1, n)), _full_spec((1, n)),
    )
    return pl.pallas_call(
        functools.partial(_proj_prompt_body, d, rows),
        grid=(t // rows,),
        in_specs=[_row_spec(rows, dm), _row_spec(rows, LANES), _row_spec(rows, LANES)]
                 + [_full_spec(w.shape) for w in weights],
        out_specs=out_specs,
        out_shape=out_shape,
        scratch_shapes=[
            pltpu.VMEM((n // LANES, rows, LANES), F32), pltpu.VMEM((n // LANES, rows, LANES), F32),
            pltpu.VMEM((groups, LANES), F32), pltpu.VMEM((groups, LANES), F32),
            pltpu.VMEM((groups, LANES), F32), pltpu.VMEM((groups, LANES), F32),
            pltpu.VMEM((2, n), F32),
        ],
        compiler_params=_params("arbitrary"),
        name="proj_mixers_prompt",
    )(h, ca, sb, *weights)


def _proj_sample(d, h, ca, sb, h0_re, h0_im, lw):
    db, dm = h.shape
    n = d.s5_lanes
    weights = [lw["g_mix"], lw["w_in"], lw["g_q"], lw["w_q"], lw["g_kv"], lw["w_ukp"],
               lw["lam"], lw["bblk"], lw["cblk"], lw["s5_d"], lw["glu_w"], lw["glu_b"],
               lw["g_gm_v"], lw["gm_w0"], lw["gm_b0"]]
    out_shape = (
        jax.ShapeDtypeStruct((db, d.qk_cols), BF16),
        jax.ShapeDtypeStruct((db, d.heads * d.kv_lora), BF16),
        jax.ShapeDtypeStruct((db, d.kv_lora), F32),
        jax.ShapeDtypeStruct((db, d.rope), F32),
        jax.ShapeDtypeStruct((db, d.s5_width), BF16),
        jax.ShapeDtypeStruct((db, d.gm_width), BF16),
        jax.ShapeDtypeStruct((db, n), F32),
        jax.ShapeDtypeStruct((db, n), F32),
        jax.ShapeDtypeStruct((db, d.gm_width), F32),
    )
    args = [h, ca, sb, *weights, h0_re, h0_im]
    return pl.pallas_call(
        functools.partial(_proj_sample_body, d),
        grid=(1,),
        in_specs=[_full_spec(a.shape) for a in args],
        out_specs=tuple(_full_spec(o.shape) for o in out_shape),
        out_shape=out_shape,
        compiler_params=_params("arbitrary"),
        name="proj_mixers_sample",
    )(*args)


def _rope_tables(d, positions):
    inv = ROPE_THETA ** (-jnp.arange(0, d.rope, 2, dtype=F32) / d.rope)
    ang = positions.astype(F32)[:, None] * inv[None, :]
    cos, sin = jnp.cos(ang), jnp.sin(ang)
    t = positions.shape[0]
    ca = jnp.concatenate([cos, cos, jnp.ones((t, d.nope), F32),
                          jnp.zeros((t, LANES - d.rope - d.nope), F32)], axis=1)
    sb = jnp.concatenate([sin, sin, jnp.zeros((t, LANES - d.rope), F32)], axis=1)
    return ca, sb


def _layer_weights(d, l, w, lam, bblk):
    f = {}
    r2 = d.rope // 2
    w_in = w["w_in"][l]
    o = np.cumsum([0, d.q_lora, d.kv_lora, d.rope, d.s5_width, d.gm_width, d.gm_width])
    c_q, c_kv, k_rope, u_s5, u_gm, v_gm = [w_in[:, o[i]:o[i + 1]] for i in range(6)]
    zpad = jnp.zeros((d.d_model, LANES - d.rope), F32)
    k_rot = jnp.concatenate([-k_rope[:, r2:], k_rope[:, :r2]], axis=1)
    f["w_in"] = jnp.concatenate([c_q, c_kv, u_s5, u_gm, v_gm, k_rope, zpad, k_rot, zpad], axis=1).astype(BF16)

    w_uq = w["w_uq"][l].reshape(d.q_lora, d.heads, d.nope + d.rope)
    qn, qr = w_uq[:, :, :d.nope], w_uq[:, :, d.nope:]
    qrot = jnp.concatenate([-qr[:, :, r2:], qr[:, :, :r2]], axis=2)
    hpad = jnp.zeros((d.q_lora, d.heads, LANES - d.rope - d.nope), F32)
    plain = jnp.concatenate([qr, qn, hpad], axis=2).reshape(d.q_lora, d.qk_cols)
    partner = jnp.concatenate([qrot, jnp.zeros_like(qn), hpad], axis=2).reshape(d.q_lora, d.qk_cols)
    f["w_q"] = jnp.concatenate([plain, partner], axis=1).astype(BF16)

    w_uk = w["w_uk"][l]
    kz0 = jnp.zeros((d.kv_lora, d.heads, d.rope), F32)
    kz1 = jnp.zeros((d.kv_lora, d.heads, LANES - d.rope - d.nope), F32)
    f["w_ka"] = jnp.concatenate([kz0, w_uk, kz1], axis=2).reshape(d.kv_lora, d.qk_cols).astype(BF16)
    f["w_ukp"] = jnp.concatenate([kz0, w_uk, kz1], axis=2).transpose(1, 2, 0).astype(BF16)
    w_uv = w["w_uv"][l]
    f["w_uvt"] = w_uv.reshape(d.kv_lora, d.mla_width).T.astype(BF16)
    eye_h = jnp.eye(d.heads, dtype=F32)
    f["w_uv_big"] = (w_uv.transpose(1, 0, 2)[:, :, None, :] * eye_h[:, None, :, None]).reshape(
        d.heads * d.kv_lora, d.mla_width).astype(BF16)

    f["lam"] = lam[l]
    f["bblk"] = bblk[l]
    eye_g = jnp.eye(d.s5_groups, dtype=F32)

    def cblock(cc):
        return (cc.transpose(0, 2, 1)[:, :, None, :] * eye_g[:, None, :, None]).reshape(d.s5_lanes, d.s5_width)

    f["cblk"] = jnp.concatenate([cblock(w["s5_c_re"][l]), -cblock(w["s5_c_im"][l])], axis=0).astype(BF16)
    f["s5_d"] = w["s5_d"][l][None, :]
    f["glu_w"] = w["s5_glu_w"][l].astype(BF16)
    f["glu_b"] = w["s5_glu_b"][l][None, :]

    f["g_gm_v"] = w["g_gm_v"][l][None, :]
    f["gm_ws"] = w["gm_ws"][l]
    gm_bs = w["gm_bs"][l]
    f["gm_bias"] = jnp.repeat(gm_bs.T, d.gm_head_dim, axis=1)
    f["gm_w0"] = jnp.repeat(w["gm_ws"][l][:, 0, 0], d.gm_head_dim)[None, :]
    f["gm_b0"] = jnp.repeat(gm_bs[:, 0], d.gm_head_dim)[None, :]

    for name in ("g_mix", "g_q", "g_kv", "g_ffn", "g_ple"):
        f[name] = w[name][l][None, :]
    for name in ("w_out", "w_up", "w_down", "w_ple", "w_pg"):
        f[name] = w[name][l].astype(BF16)
    return f


def kernel(x_prompt, x_sample, cache_ckv, cache_krope, state_s5_re, state_s5_im, page_table, p_prompt, p_sample, g_mix, w_in, g_q, w_uq, g_kv, w_uk, w_uv, s5_a_re, s5_a_im, s5_log_dt, s5_b_re, s5_b_im, s5_c_re, s5_c_im, s5_d, s5_glu_w, s5_glu_b, g_gm_v, gm_ws, gm_bs, w_out, g_ffn, w_up, w_down, g_ple, w_ple, w_pg, g_final):
    w = dict(g_mix=g_mix, w_in=w_in, g_q=g_q, w_uq=w_uq, g_kv=g_kv, w_uk=w_uk, w_uv=w_uv,
             s5_c_re=s5_c_re, s5_c_im=s5_c_im, s5_d=s5_d, s5_glu_w=s5_glu_w, s5_glu_b=s5_glu_b,
             g_gm_v=g_gm_v, gm_ws=gm_ws, gm_bs=gm_bs, w_out=w_out, g_ffn=g_ffn, w_up=w_up,
             w_down=w_down, g_ple=g_ple, w_ple=w_ple, w_pg=w_pg)
    d = _Dims(w_in, w_uq, w_uk, w_uv, s5_b_re, gm_ws)
    depth = w_in.shape[0]
    batch, seq, dm = x_prompt.shape
    db, dec_seq, _ = x_sample.shape
    assert batch == 1 and dec_seq == 1
    past_len = page_table.shape[1] * cache_ckv.shape[2]

    lam, bblk = _s5_discretise(s5_a_re, s5_a_im, s5_log_dt, s5_b_re, s5_b_im)
    ca_p, sb_p = _rope_tables(d, jnp.arange(seq))
    ca_s, sb_s = _rope_tables(d, jnp.full((db,), past_len))
    g_fin = g_final[None, :]
    cache_krope_t = jnp.swapaxes(cache_krope, 2, 3)

    hp = x_prompt.reshape(seq, dm)
    hs = x_sample.reshape(db, dm)
    outs = [[] for _ in range(9)]
    for l in range(depth):
        lw = _layer_weights(d, l, w, lam, bblk)
        last = l == depth - 1
        q, k, vt, ckv_p, kr_p, os5, ogm, sre_p, sim_p = _proj_prompt(d, hp, ca_p, sb_p, lw)
        att = _flash_attention(d, q, k, vt)
        hp = _out_layer(d, hp, att, os5, ogm, p_prompt[l].reshape(seq, -1), None, lw, g_fin if last else None)
        qs, qlat, ckv_s, kr_s, os5_s, ogm_s, sre_s, sim_s, vn_s = _proj_sample(
            d, hs, ca_s, sb_s, state_s5_re[l].reshape(db, -1), state_s5_im[l].reshape(db, -1), lw)
        olat = _paged_attention(d, l, page_table, qlat, qs, ckv_s, kr_s, cache_ckv, cache_krope_t)
        hs = _out_layer(d, hs, olat.reshape(db, -1), os5_s, ogm_s, p_sample[l].reshape(db, -1),
                        lw["w_uv_big"], lw, g_fin if last else None)
        gshape = (d.s5_groups, d.s5_state)
        for lst, val in zip(outs, (
                ckv_p.reshape(batch, seq, -1), kr_p.reshape(batch, seq, -1),
                ckv_s.reshape(db, dec_seq, -1), kr_s.reshape(db, dec_seq, -1),
                sre_p.reshape(batch, *gshape), sim_p.reshape(batch, *gshape),
                sre_s.reshape(db, *gshape), sim_s.reshape(db, *gshape),
                vn_s.reshape(db, dec_seq, -1))):
            lst.append(val)
    return (hp.reshape(batch, seq, dm), hs.reshape(db, dec_seq, dm), *[jnp.stack(o) for o in outs])
```

```python
import functools
import math

import numpy as np
import jax
import jax.numpy as jnp
from jax import lax
from jax.experimental import pallas as pl
from jax.experimental.pallas import tpu as pltpu

F32 = jnp.float32
BF16 = jnp.bfloat16

EPS = 1e-6
ROPE_THETA = 10000.0
NEG_INIT = -1e30
PAGE_SIZE = 128

LANES = 128
SUBLANES = 8
BF16_SUBLANES = 16
VMEM_LIMIT_BYTES = 56 * 1024 * 1024

PROJ_ROWS = 256
OUT_ROWS = 512
FLASH_KEY_BLOCK = 512
FLASH_QUERY_GROUPS = 4
FFN_CHUNK = 1024
PAGES_PER_STEP = 32
PAGED_SLOTS = 4


def _dot(a, b):
    return jnp.dot(a, b, preferred_element_type=F32)


def _dot_nt(a, b):
    return lax.dot_general(a, b, (((1,), (1,)), ((), ())), preferred_element_type=F32)


def _rms(x, g):
    return x * lax.rsqrt(jnp.mean(x * x, axis=-1, keepdims=True) + EPS) * g


def _full_spec(shape):
    nd = len(shape)
    return pl.BlockSpec(shape, lambda *_: (0,) * nd)


def _row_spec(rows, cols):
    return pl.BlockSpec((rows, cols), lambda i, *_: (i, 0))


def _params(*sem):
    return pltpu.CompilerParams(dimension_semantics=sem, vmem_limit_bytes=VMEM_LIMIT_BYTES)


def _s5_discretise_body(are_ref, aim_ref, ldt_ref, bre_ref, bim_ref, abre_ref, abim_ref, bbre_ref, bbim_ref):
    a_re = are_ref[...]
    a_im = aim_ref[...]
    dt = jnp.exp(ldt_ref[...])
    mag = jnp.exp(a_re * dt)
    ab_re = mag * jnp.cos(a_im * dt)
    ab_im = mag * jnp.sin(a_im * dt)
    den = a_re * a_re + a_im * a_im
    n_re = ab_re - 1.0
    n_im = ab_im
    k_re = (n_re * a_re + n_im * a_im) / den
    k_im = (n_im * a_re - n_re * a_im) / den
    b_re = bre_ref[...]
    b_im = bim_ref[...]
    abre_ref[...] = ab_re
    abim_ref[...] = ab_im
    bbre_ref[...] = k_re * b_re - k_im * b_im
    bbim_ref[...] = k_re * b_im + k_im * b_re


def _s5_discretise(a_re, a_im, log_dt, b_re, b_im):
    L, G, P, CH = b_re.shape
    rows = L * G * CH
    rep = lambda x: jnp.broadcast_to(x[:, :, None, :], (L, G, CH, P)).reshape(rows, P)
    a_re_r, a_im_r = rep(a_re), rep(a_im)
    ldt_r = jnp.broadcast_to(log_dt[:, :, None, None], (L, G, CH, P)).reshape(rows, P)
    bt_re = b_re.transpose(0, 1, 3, 2).reshape(rows, P)
    bt_im = b_im.transpose(0, 1, 3, 2).reshape(rows, P)
    out = jax.ShapeDtypeStruct((rows, P), F32)
    ab_re, ab_im, bb_re, bb_im = pl.pallas_call(
        _s5_discretise_body, out_shape=(out, out, out, out), name="s5_discretise")(
            a_re_r, a_im_r, ldt_r, bt_re, bt_im)
    lam = jnp.stack([ab_re.reshape(L, G, CH, P)[:, :, 0, :].reshape(L, G * P),
                     ab_im.reshape(L, G, CH, P)[:, :, 0, :].reshape(L, G * P)], axis=1)
    eye = jnp.eye(G, dtype=F32)

    def blockdiag(bb):
        bb = bb.reshape(L, G, CH, P)
        return (bb[:, :, :, None, :] * eye[None, :, None, :, None]).reshape(L, G * CH, G * P)

    bblk = jnp.concatenate([blockdiag(bb_re), blockdiag(bb_im)], axis=-1).astype(BF16)
    return lam, bblk


class _Dims:
    def __init__(self, w_in, w_uq, w_uk, w_uv, s5_b_re, gm_ws):
        self.d_model = w_in.shape[1]
        self.kv_lora = w_uk.shape[1]
        self.heads = w_uk.shape[2]
        self.nope = w_uk.shape[3]
        self.v_dim = w_uv.shape[3]
        self.q_lora = w_uq.shape[1]
        self.rope = w_uq.shape[2] // self.heads - self.nope
        self.s5_groups, self.s5_state, self.s5_ch = s5_b_re.shape[1:]
        self.s5_width = self.s5_groups * self.s5_ch
        self.s5_lanes = self.s5_groups * self.s5_state
        self.gm_heads = gm_ws.shape[1]
        self.chunk = gm_ws.shape[2]
        self.gm_width = (w_in.shape[2] - self.q_lora - self.kv_lora - self.rope - self.s5_width) // 2
        self.gm_head_dim = self.gm_width // self.gm_heads
        self.mla_width = self.heads * self.v_dim
        self.v_slab = self.v_dim + BF16_SUBLANES
        self.attn_scale = 1.0 / math.sqrt(self.nope + self.rope)
        assert self.rope + self.nope <= LANES and self.rope % 2 == 0
        self.qk_cols = self.heads * LANES
        self.o_cq = 0
        self.o_ckv = self.o_cq + self.q_lora
        self.o_s5 = self.o_ckv + self.kv_lora
        self.o_ugm = self.o_s5 + self.s5_width
        self.o_vgm = self.o_ugm + self.gm_width
        self.o_kra = self.o_vgm + self.gm_width
        self.o_krb = self.o_kra + LANES
        self.in_cols = self.o_krb + LANES
        for off in (self.o_ckv, self.o_s5, self.o_ugm, self.o_vgm, self.o_kra):
            assert off % LANES == 0


def _project_common(d, q_scale, h_ref, ca_ref, sb_ref, gmix_ref, win_ref, gq_ref, wq_ref, gkv_ref,
                    q_ref, ckv_ref, kr_ref, after_in_proj=None):
    a = _rms(h_ref[...], gmix_ref[...]).astype(BF16)
    z = _dot(a, win_ref[...])
    if after_in_proj is not None:
        after_in_proj(z)
    ca = ca_ref[...]
    sb = sb_ref[...]
    cqn = _rms(z[:, d.o_cq:d.o_cq + d.q_lora], gq_ref[...]).astype(BF16)
    qq = _dot(cqn, wq_ref[...])
    for hh in range(d.heads):
        lo = hh * LANES
        q_ref[:, lo:lo + LANES] = (
            (qq[:, lo:lo + LANES] * ca + qq[:, d.qk_cols + lo:d.qk_cols + lo + LANES] * sb)
            * q_scale).astype(BF16)
    ckv = _rms(z[:, d.o_ckv:d.o_ckv + d.kv_lora], gkv_ref[...])
    ckv_ref[...] = ckv
    kr_slot = z[:, d.o_kra:d.o_kra + LANES] * ca + z[:, d.o_krb:d.o_krb + LANES] * sb
    kr_ref[...] = kr_slot[:, :d.rope]
    u_s5 = z[:, d.o_s5:d.o_s5 + d.s5_width]
    u_gm = z[:, d.o_ugm:d.o_ugm + d.gm_width]
    v_gm = z[:, d.o_vgm:d.o_vgm + d.gm_width]
    return q_ref, ckv, kr_slot, u_s5, u_gm, v_gm


def _s5_readout(d, hr_b, hi_b, u_s5, cblk_ref, dskip_ref, gluw_ref, glub_ref):
    n = d.s5_lanes
    y = _dot(hr_b, cblk_ref[0:n, :]) + _dot(hi_b, cblk_ref[n:2 * n, :]) + dskip_ref[...] * u_s5
    g = jax.nn.gelu(y)
    zz = _dot(g.astype(BF16), gluw_ref[...]) + glub_ref[...]
    w = d.s5_width
    return zz[:, :w] * jax.nn.sigmoid(zz[:, w:])


def _cmul_add(ar, ai, br, bi, cr, ci):
    return ar * br - ai * bi + cr, ar * bi + ai * br + ci


def _proj_prompt_body(d, rows,
                      h_ref, ca_ref, sb_ref, gmix_ref, win_ref, gq_ref, wq_ref, gkv_ref, wka_ref, wuvt_ref,
                      lam_ref, bblk_ref, cblk_ref, dskip_ref, gluw_ref, glub_ref, ggm_ref, gws_ref, gbias_ref,
                      q_ref, k_ref, vt_ref, ckv_ref, kr_ref, os5_ref, ogm_ref, sre_ref, sim_ref,
                      xr_scr, xi_scr, er_scr, ei_scr, cr_scr, ci_scr, st_scr):
    @pl.when(pl.program_id(0) == 0)
    def _():
        st_scr[...] = jnp.zeros_like(st_scr)

    n = d.s5_lanes
    n_slabs = n // LANES

    def s5_input(z):
        bu = _dot(z[:, d.o_s5:d.o_s5 + d.s5_width].astype(BF16), bblk_ref[...])
        for s in range(n_slabs):
            xr_scr[s] = bu[:, s * LANES:(s + 1) * LANES]
            xi_scr[s] = bu[:, n + s * LANES:n + (s + 1) * LANES]

    _, ckv, kr_slot, u_s5, u_gm, v_gm = _project_common(
        d, d.attn_scale * math.log2(math.e),
        h_ref, ca_ref, sb_ref, gmix_ref, win_ref, gq_ref, wq_ref, gkv_ref, q_ref, ckv_ref, kr_ref,
        after_in_proj=s5_input)

    ckvb = ckv.astype(BF16)
    kk = _dot(ckvb, wka_ref[...])
    for hh in range(d.heads):
        lo = hh * LANES
        k_ref[:, lo:lo + LANES] = (kk[:, lo:lo + LANES] + kr_slot).astype(BF16)
    vt = _dot_nt(wuvt_ref[...], ckvb).astype(BF16)
    ones_tile = (lax.broadcasted_iota(jnp.int32, (BF16_SUBLANES, rows), 0) == 0).astype(BF16)
    for hh in range(d.heads):
        vlo = hh * d.v_slab
        vt_ref[vlo:vlo + d.v_dim, :] = vt[hh * d.v_dim:(hh + 1) * d.v_dim, :]
        vt_ref[vlo + d.v_dim:vlo + d.v_slab, :] = ones_tile

    vn = _rms(v_gm, ggm_ref[...])
    c = d.chunk
    tril = lax.broadcasted_iota(jnp.int32, (c, c), 1) <= lax.broadcasted_iota(jnp.int32, (c, c), 0)
    head_of_lane = lax.broadcasted_iota(jnp.int32, (c, d.gm_width), 1) // d.gm_head_dim
    w_low = [jnp.where(tril, gws_ref[g], 0.0).astype(BF16) for g in range(d.gm_heads)]
    for cc in range(rows // c):
        vc = vn[cc * c:(cc + 1) * c, :]
        s = gbias_ref[...]
        for g in range(d.gm_heads):
            s = s + _dot(w_low[g], jnp.where(head_of_lane == g, vc, 0.0).astype(BF16))
        ogm_ref[cc * c:(cc + 1) * c, :] = (u_gm[cc * c:(cc + 1) * c, :] * s).astype(BF16)

    groups = rows // SUBLANES
    for s in range(n_slabs):
        lo = s * LANES
        xr = xr_scr.at[s]
        xi = xi_scr.at[s]
        lam_r = lam_ref[0:1, lo:lo + LANES]
        lam_i = lam_ref[1:2, lo:lo + LANES]
        pw_r, pw_i = [lam_r], [lam_i]
        for _ in range(SUBLANES - 1):
            nr, ni = _cmul_add(pw_r[-1], pw_i[-1], lam_r, lam_i, 0.0, 0.0)
            pw_r.append(nr)
            pw_i.append(ni)
        hr = xr[pl.ds(0, groups, stride=SUBLANES), :]
        hi = xi[pl.ds(0, groups, stride=SUBLANES), :]
        for r in range(1, SUBLANES):
            hr, hi = _cmul_add(lam_r, lam_i, hr, hi,
                               xr[pl.ds(r, groups, stride=SUBLANES), :],
                               xi[pl.ds(r, groups, stride=SUBLANES), :])
            xr[pl.ds(r, groups, stride=SUBLANES), :] = hr
            xi[pl.ds(r, groups, stride=SUBLANES), :] = hi
        er_scr[...] = hr
        ei_scr[...] = hi
        cr = st_scr[0:1, lo:lo + LANES]
        ci = st_scr[1:2, lo:lo + LANES]
        for j in range(groups):
            cr_scr[j:j + 1, :] = cr
            ci_scr[j:j + 1, :] = ci
            cr, ci = _cmul_add(pw_r[-1], pw_i[-1], cr, ci, er_scr[j:j + 1, :], ei_scr[j:j + 1, :])
        st_scr[0:1, lo:lo + LANES] = cr
        st_scr[1:2, lo:lo + LANES] = ci
        cin_r = cr_scr[...]
        cin_i = ci_scr[...]
        for r in range(SUBLANES):
            hr, hi = _cmul_add(pw_r[r], pw_i[r], cin_r, cin_i,
                               xr[pl.ds(r, groups, stride=SUBLANES), :],
                               xi[pl.ds(r, groups, stride=SUBLANES), :])
            xr[pl.ds(r, groups, stride=SUBLANES), :] = hr
            xi[pl.ds(r, groups, stride=SUBLANES), :] = hi
    sre_ref[...] = st_scr[0:1, :]
    sim_ref[...] = st_scr[1:2, :]
    h_re = jnp.concatenate([xr_scr[s] for s in range(n // LANES)], axis=1).astype(BF16)
    h_im = jnp.concatenate([xi_scr[s] for s in range(n // LANES)], axis=1).astype(BF16)
    o_s5 = _s5_readout(d, h_re, h_im, u_s5,
                       cblk_ref, dskip_ref, gluw_ref, glub_ref)
    os5_ref[...] = o_s5.astype(BF16)


def _proj_sample_body(d,
                      h_ref, ca_ref, sb_ref, gmix_ref, win_ref, gq_ref, wq_ref, gkv_ref, wukp_ref,
                      lam_ref, bblk_ref, cblk_ref, dskip_ref, gluw_ref, glub_ref, ggm_ref, gw0_ref, gb0_ref,
                      h0r_ref, h0i_ref,
                      q_ref, qlat_ref, ckv_ref, kr_ref, os5_ref, ogm_ref, sre_ref, sim_ref, vn_ref):
    _, _, _, u_s5, u_gm, v_gm = _project_common(
        d, d.attn_scale,
        h_ref, ca_ref, sb_ref, gmix_ref, win_ref, gq_ref, wq_ref, gkv_ref, q_ref, ckv_ref, kr_ref)

    for hh in range(d.heads):
        qlat_ref[:, hh * d.kv_lora:(hh + 1) * d.kv_lora] = _dot(
            q_ref[:, hh * LANES:(hh + 1) * LANES], wukp_ref[hh]).astype(BF16)

    n = d.s5_lanes
    bu = _dot(u_s5.astype(BF16), bblk_ref[...])
    hr, hi = _cmul_add(lam_ref[0:1, :], lam_ref[1:2, :], h0r_ref[...], h0i_ref[...], bu[:, :n], bu[:, n:])
    sre_ref[...] = hr
    sim_ref[...] = hi
    o_s5 = _s5_readout(d, hr.astype(BF16), hi.astype(BF16), u_s5, cblk_ref, dskip_ref, gluw_ref, glub_ref)
    os5_ref[...] = o_s5.astype(BF16)

    vn = _rms(v_gm, ggm_ref[...])
    vn_ref[...] = vn
    ogm_ref[...] = (u_gm * (gw0_ref[...] * vn + gb0_ref[...])).astype(BF16)


def _flash_body(d, kb, groups, qi_ref, ki_ref, q_ref, k_ref, vt_ref, o_ref, m_scr, acc_scr, s_scr):
    step = pl.program_id(0)
    ki = ki_ref[step]
    rel = ki - qi_ref[step] * groups

    @pl.when(ki == 0)
    def _():
        m_scr[...] = jnp.full_like(m_scr, NEG_INIT)
        acc_scr[...] = jnp.zeros_like(acc_scr)

    def scores(unit, slot):
        hh, g = unit
        lo = hh * LANES
        s_scr[slot] = _dot_nt(k_ref[:, lo:lo + LANES], q_ref[g * kb:(g + 1) * kb, lo:lo + LANES])

    def update(diag_group):
        first = 0 if diag_group is None else diag_group
        units = [(hh, g) for hh in range(d.heads) for g in range(first, groups)]
        scores(units[0], 0)
        for u, (hh, g) in enumerate(units):
            if u + 1 < len(units):
                scores(units[u + 1], (u + 1) % 2)
            st = s_scr[u % 2]
            if g == diag_group:
                keep = (lax.broadcasted_iota(jnp.int32, (kb, kb), 0)
                        <= lax.broadcasted_iota(jnp.int32, (kb, kb), 1))
                st = jnp.where(keep, st, NEG_INIT)
            cols = slice(g * kb, (g + 1) * kb)
            m_prev = m_scr[hh:hh + 1, cols]
            m_new = jnp.maximum(m_prev, jnp.max(st, axis=0, keepdims=True))
            alpha = jnp.exp2(m_prev - m_new)
            p = jnp.exp2(st - m_new).astype(BF16)
            vlo = hh * d.v_slab
            acc_scr[vlo:vlo + d.v_slab, cols] = alpha * acc_scr[vlo:vlo + d.v_slab, cols] + _dot(
                vt_ref[vlo:vlo + d.v_slab, :], p)
            m_scr[hh:hh + 1, cols] = m_new

    @pl.when(rel < 0)
    def _():
        update(None)

    for dg in range(groups):
        @pl.when(rel == dg)
        def _():
            update(dg)

    @pl.when(rel == groups - 1)
    def _():
        outs = []
        for hh in range(d.heads):
            vlo = hh * d.v_slab
            outs.append(acc_scr[vlo:vlo + d.v_dim, :] / acc_scr[vlo + d.v_dim:vlo + d.v_dim + 1, :])
        o_ref[...] = jnp.concatenate(outs, axis=0).T.astype(BF16)


def _flash_attention(d, q, k, vt):
    t = q.shape[0]
    kb = min(FLASH_KEY_BLOCK, t)
    groups = min(FLASH_QUERY_GROUPS, t // kb)
    qb = kb * groups
    assert t % qb == 0
    nq = t // qb
    qi = np.concatenate([np.full((i + 1) * groups, i, np.int32) for i in range(nq)])
    ki = np.concatenate([np.arange((i + 1) * groups, dtype=np.int32) for i in range(nq)])
    grid_spec = pltpu.PrefetchScalarGridSpec(
        num_scalar_prefetch=2,
        grid=(len(qi),),
        in_specs=[
            pl.BlockSpec((qb, d.qk_cols), lambda s, qi, ki: (qi[s], 0)),
            pl.BlockSpec((kb, d.qk_cols), lambda s, qi, ki: (ki[s], 0)),
            pl.BlockSpec((d.heads * d.v_slab, kb), lambda s, qi, ki: (0, ki[s])),
        ],
        out_specs=pl.BlockSpec((qb, d.mla_width), lambda s, qi, ki: (qi[s], 0)),
        scratch_shapes=[
            pltpu.VMEM((d.heads, qb), F32),
            pltpu.VMEM((d.heads * d.v_slab, qb), F32),
            pltpu.VMEM((2, kb, kb), F32),
        ],
    )
    return pl.pallas_call(
        functools.partial(_flash_body, d, kb, groups),
        grid_spec=grid_spec,
        out_shape=jax.ShapeDtypeStruct((t, d.mla_width), BF16),
        compiler_params=_params("arbitrary"),
        name="prompt_flash_attention",
    )(jnp.asarray(qi), jnp.asarray(ki), q, k, vt)


def _paged_body(d, layer, pages, n_chunks, n_rows, pt_ref, qlat_ref, q_ref, ckvn_ref, krn_ref, kc_hbm, kp_hbm,
                o_ref, kc_buf, kp_buf, sem, kcb_scr, p_scr, corr_scr, m_scr, l_scr, acc_scr):
    b = pl.program_id(0)
    rows = pl.num_programs(0)

    def chunk_copies(row, c, slot, page_range=range(pages)):
        cps = []
        for j in page_range:
            pg = pt_ref[row, c * pages + j]
            cps.append(pltpu.make_async_copy(kc_hbm.at[layer, pg], kc_buf.at[slot, j], sem.at[0, slot]))
            cps.append(pltpu.make_async_copy(kp_hbm.at[layer, pg], kp_buf.at[slot, j], sem.at[1, slot]))
        return cps

    def start_all(cps):
        for i, cp in enumerate(cps):
            cp.start(priority=(i // 2) % 2)

    ahead = PAGED_SLOTS - 1
    assert ahead <= n_chunks

    @pl.when(b == 0)
    def _():
        for g in range(ahead):
            start_all(chunk_copies(0, g, g % PAGED_SLOTS))

        kcb_scr[1] = jnp.zeros(kcb_scr.shape[1:], BF16)

    ql = qlat_ref[...]
    qr = q_ref[:, 0:d.rope]
    m_scr[...] = jnp.full_like(m_scr, NEG_INIT)
    l_scr[...] = jnp.zeros_like(l_scr)
    acc_scr[...] = jnp.zeros_like(acc_scr)
    p_scr[...] = jnp.zeros_like(p_scr)
    corr_scr[...] = jnp.ones_like(corr_scr)
    tile_pages = 2
    tile_keys = tile_pages * PAGE_SIZE
    n_tiles = pages // tile_pages

    def value_matmuls(kslot):
        pb = p_scr[...]
        pv = None
        for t in range(n_tiles):
            part = _dot(pb[:, t * tile_keys:(t + 1) * tile_keys],
                        kcb_scr[kslot, t * tile_keys:(t + 1) * tile_keys, :])
            pv = part if pv is None else pv + part
        acc_scr[...] = acc_scr[...] * corr_scr[...] + pv

    def chunk(c, carry):
        g = b * n_chunks + c
        slot = lax.rem(g, PAGED_SLOTS)
        wraps = c + ahead >= n_chunks
        nxt_row = jnp.minimum(jnp.where(wraps, b + 1, b), rows - 1)
        nxt_c = jnp.where(wraps, c + ahead - n_chunks, c + ahead)
        nxt_slot = lax.rem(g + ahead, PAGED_SLOTS)

        for cp in chunk_copies(b, c, slot):
            cp.wait()

        kslot = c % 2
        s_tiles = []
        for t in range(n_tiles):
            js = range(t * tile_pages, (t + 1) * tile_pages)
            kcb = jnp.concatenate([kc_buf[slot, j].astype(BF16) for j in js], axis=0)
            kpb = jnp.concatenate([kp_buf[slot, j].astype(BF16) for j in js], axis=1)
            kcb_scr[kslot, t * tile_keys:(t + 1) * tile_keys, :] = kcb
            s_tiles.append(_dot_nt(ql, kcb) + _dot(qr, kpb))
            start_all(chunk_copies(nxt_row, nxt_c, nxt_slot, js))
        value_matmuls(1 - kslot)
        s = jnp.concatenate(s_tiles, axis=1)
        m_prev = m_scr[...]
        m_new = jnp.maximum(m_prev, jnp.max(s, axis=-1, keepdims=True))
        corr = jnp.exp(m_prev - m_new)
        p = jnp.exp(s - m_new)
        l_scr[...] = l_scr[...] * corr + jnp.sum(p, axis=-1, keepdims=True)
        p_scr[...] = p.astype(BF16)
        corr_scr[...] = corr
        m_scr[...] = m_new
        return carry

    lax.fori_loop(0, n_chunks, chunk, 0)
    value_matmuls((n_chunks - 1) % 2)

    @pl.when(b == rows - 1)
    def _():
        for extra in range(ahead):
            for cp in chunk_copies(n_rows - 1, extra, (n_rows * n_chunks + extra) % PAGED_SLOTS):
                cp.wait()

    cn = ckvn_ref[...]
    s_self = (jnp.sum(ql.astype(F32) * cn, axis=-1, keepdims=True)
              + jnp.sum(qr.astype(F32) * krn_ref[...], axis=-1, keepdims=True))
    m_prev = m_scr[...]
    m_new = jnp.maximum(m_prev, s_self)
    corr = jnp.exp(m_prev - m_new)
    p = jnp.exp(s_self - m_new)
    l = l_scr[...] * corr + p
    acc = acc_scr[...] * corr + p * cn
    o_ref[...] = (acc / l).astype(BF16)


def _paged_attention(d, layer, page_table, q_lat, q, ckv_new, kr_new, cache_ckv, cache_krope_t):
    db, n_pages = page_table.shape
    pages = min(PAGES_PER_STEP, n_pages)
    assert n_pages % pages == 0 and cache_ckv.shape[2] == PAGE_SIZE

    assert pages % 2 == 0 and (n_pages // pages) % 2 == 0
    grid_spec = pltpu.PrefetchScalarGridSpec(
        num_scalar_prefetch=1,
        grid=(db,),
        in_specs=[
            pl.BlockSpec((None, d.heads, d.kv_lora), lambda b, pt: (b, 0, 0)),
            pl.BlockSpec((None, d.heads, LANES), lambda b, pt: (b, 0, 0)),
            pl.BlockSpec((None, 1, d.kv_lora), lambda b, pt: (b, 0, 0)),
            pl.BlockSpec((None, 1, d.rope), lambda b, pt: (b, 0, 0)),
            pl.BlockSpec(memory_space=pl.ANY),
            pl.BlockSpec(memory_space=pl.ANY),
        ],
        out_specs=pl.BlockSpec((None, d.heads, d.kv_lora), lambda b, pt: (b, 0, 0)),
        scratch_shapes=[
            pltpu.VMEM((PAGED_SLOTS, pages, PAGE_SIZE, d.kv_lora), F32),
            pltpu.VMEM((PAGED_SLOTS, pages, d.rope, PAGE_SIZE), F32),
            pltpu.SemaphoreType.DMA((2, PAGED_SLOTS)),
            pltpu.VMEM((2, pages * PAGE_SIZE, d.kv_lora), BF16),
            pltpu.VMEM((d.heads, pages * PAGE_SIZE), BF16),
            pltpu.VMEM((d.heads, 1), F32),
            pltpu.VMEM((d.heads, 1), F32),
            pltpu.VMEM((d.heads, 1), F32),
            pltpu.VMEM((d.heads, d.kv_lora), F32),
        ],
    )
    return pl.pallas_call(
        functools.partial(_paged_body, d, layer, pages, n_pages // pages, db),
        grid_spec=grid_spec,
        out_shape=jax.ShapeDtypeStruct((db, d.heads, d.kv_lora), BF16),
        compiler_params=_params("arbitrary"),
        name="decode_paged_attention",
    )(page_table,
      q_lat.reshape(db, d.heads, d.kv_lora), q.reshape(db, d.heads, LANES),
      ckv_new.reshape(db, 1, d.kv_lora), kr_new.reshape(db, 1, d.rope),
      cache_ckv, cache_krope_t)


def _out_body(d, latent_attn, final_norm, d_ff, *refs):
    refs = list(refs)
    h_ref, att_ref, os5_ref, ogm_ref, ple_ref = refs[:5]
    refs = refs[5:]
    if latent_attn:
        wuvb_ref = refs.pop(0)
    wout_ref, gffn_ref, wup_ref, wdown_ref, gple_ref, wple_ref, wpg_ref = refs[:7]
    refs = refs[7:]
    if final_norm:
        gfin_ref = refs.pop(0)
    out_ref, = refs

    if latent_attn:
        o_mla = _dot(att_ref[...], wuvb_ref[...]).astype(BF16)
    else:
        o_mla = att_ref[...]
    m0 = d.mla_width
    m1 = m0 + d.s5_width
    m2 = m1 + d.gm_width
    h = (h_ref[...] + _dot(o_mla, wout_ref[0:m0, :]) + _dot(os5_ref[...], wout_ref[m0:m1, :])
         + _dot(ogm_ref[...], wout_ref[m1:m2, :]))
    a = _rms(h, gffn_ref[...]).astype(BF16)
    fc = min(FFN_CHUNK, d_ff)
    ffn = None
    for cc in range(d_ff // fc):
        f = jnp.maximum(_dot(a, wup_ref[:, cc * fc:(cc + 1) * fc]), 0.0)
        part = _dot((f * f).astype(BF16), wdown_ref[cc * fc:(cc + 1) * fc, :])
        ffn = part if ffn is None else ffn + part
    h = h + ffn
    gate = jax.nn.sigmoid(_dot(_rms(h, gple_ref[...]).astype(BF16), wpg_ref[...]))
    h = h + _dot(ple_ref[...].astype(BF16), wple_ref[...]) * gate
    if final_norm:
        h = _rms(h, gfin_ref[...])
    out_ref[...] = h


def _out_layer(d, h, att, o_s5, o_gm, ple, wuv_big, lw, g_final):
    t, dm = h.shape
    rows = min(OUT_ROWS, t)
    assert t % rows == 0
    d_ff = lw["w_up"].shape[1]
    latent_attn = wuv_big is not None
    final_norm = g_final is not None
    def wspec(x):
        return pl.BlockSpec(x.shape, lambda i: (0,) * x.ndim, pipeline_mode=pl.Buffered(1))

    args = [h, att, o_s5, o_gm, ple]
    specs = [_row_spec(rows, dm), _row_spec(rows, att.shape[1]), _row_spec(rows, o_s5.shape[1]),
             _row_spec(rows, o_gm.shape[1]), _row_spec(rows, ple.shape[1])]
    weights = ([wuv_big] if latent_attn else []) + [
        lw["w_out"], lw["g_ffn"], lw["w_up"], lw["w_down"], lw["g_ple"], lw["w_ple"], lw["w_pg"]]
    if final_norm:
        weights.append(g_final)
    return pl.pallas_call(
        functools.partial(_out_body, d, latent_attn, final_norm, d_ff),
        grid=(t // rows,),
        in_specs=specs + [wspec(w) for w in weights],
        out_specs=_row_spec(rows, dm),
        out_shape=jax.ShapeDtypeStruct((t, dm), F32),
        compiler_params=_params("parallel"),
        name="out_ffn_ple",
    )(*args, *weights)


def _proj_prompt(d, h, ca, sb, lw):
    t, dm = h.shape
    rows = min(PROJ_ROWS, t)
    assert t % rows == 0 and rows % d.chunk == 0 and rows % SUBLANES == 0
    groups = rows // SUBLANES
    n = d.s5_lanes
    weights = [lw["g_mix"], lw["w_in"], lw["g_q"], lw["w_q"], lw["g_kv"], lw["w_ka"], lw["w_uvt"],
               lw["lam"], lw["bblk"], lw["cblk"], lw["s5_d"], lw["glu_w"], lw["glu_b"],
               lw["g_gm_v"], lw["gm_ws"], lw["gm_bias"]]
    out_shape = (
        jax.ShapeDtypeStruct((t, d.qk_cols), BF16),
        jax.ShapeDtypeStruct((t, d.qk_cols), BF16),
        jax.ShapeDtypeStruct((d.heads * d.v_slab, t), BF16),
        jax.ShapeDtypeStruct((t, d.kv_lora), F32),
        jax.ShapeDtypeStruct((t, d.rope), F32),
        jax.ShapeDtypeStruct((t, d.s5_width), BF16),
        jax.ShapeDtypeStruct((t, d.gm_width), BF16),
        jax.ShapeDtypeStruct((1, n), F32),
        jax.ShapeDtypeStruct((1, n), F32),
    )
    out_specs = (
        _row_spec(rows, d.qk_cols), _row_spec(rows, d.qk_cols),
        pl.BlockSpec((d.heads * d.v_slab, rows), lambda i: (0, i)),
        _row_spec(rows, d.kv_lora), _row_spec(rows, d.rope),
        _row_spec(rows, d.s5_width), _row_spec(rows, d.gm_width),
        _full_spec((1, n)), _full_spec((1, n)),
    )
    return pl.pallas_call(
        functools.partial(_proj_prompt_body, d, rows),
        grid=(t // rows,),
        in_specs=[_row_spec(rows, dm), _row_spec(rows, LANES), _row_spec(rows, LANES)]
                 + [_full_spec(w.shape) for w in weights],
        out_specs=out_specs,
        out_shape=out_shape,
        scratch_shapes=[
            pltpu.VMEM((n // LANES, rows, LANES), F32), pltpu.VMEM((n // LANES, rows, LANES), F32),
            pltpu.VMEM((groups, LANES), F32), pltpu.VMEM((groups, LANES), F32),
            pltpu.VMEM((groups, LANES), F32), pltpu.VMEM((groups, LANES), F32),
            pltpu.VMEM((2, n), F32),
        ],
        compiler_params=_params("arbitrary"),
        name="proj_mixers_prompt",
    )(h, ca, sb, *weights)


def _proj_sample(d, h, ca, sb, h0_re, h0_im, lw):
    db, dm = h.shape
    n = d.s5_lanes
    weights = [lw["g_mix"], lw["w_in"], lw["g_q"], lw["w_q"], lw["g_kv"], lw["w_ukp"],
               lw["lam"], lw["bblk"], lw["cblk"], lw["s5_d"], lw["glu_w"], lw["glu_b"],
               lw["g_gm_v"], lw["gm_w0"], lw["gm_b0"]]
    out_shape = (
        jax.ShapeDtypeStruct((db, d.qk_cols), BF16),
        jax.ShapeDtypeStruct((db, d.heads * d.kv_lora), BF16),
        jax.ShapeDtypeStruct((db, d.kv_lora), F32),
        jax.ShapeDtypeStruct((db, d.rope), F32),
        jax.ShapeDtypeStruct((db, d.s5_width), BF16),
        jax.ShapeDtypeStruct((db, d.gm_width), BF16),
        jax.ShapeDtypeStruct((db, n), F32),
        jax.ShapeDtypeStruct((db, n), F32),
        jax.ShapeDtypeStruct((db, d.gm_width), F32),
    )
    args = [h, ca, sb, *weights, h0_re, h0_im]
    return pl.pallas_call(
        functools.partial(_proj_sample_body, d),
        grid=(1,),
        in_specs=[_full_spec(a.shape) for a in args],
        out_specs=tuple(_full_spec(o.shape) for o in out_shape),
        out_shape=out_shape,
        compiler_params=_params("arbitrary"),
        name="proj_mixers_sample",
    )(*args)


def _rope_tables(d, positions):
    inv = ROPE_THETA ** (-jnp.arange(0, d.rope, 2, dtype=F32) / d.rope)
    ang = positions.astype(F32)[:, None] * inv[None, :]
    cos, sin = jnp.cos(ang), jnp.sin(ang)
    t = positions.shape[0]
    ca = jnp.concatenate([cos, cos, jnp.ones((t, d.nope), F32),
                          jnp.zeros((t, LANES - d.rope - d.nope), F32)], axis=1)
    sb = jnp.concatenate([sin, sin, jnp.zeros((t, LANES - d.rope), F32)], axis=1)
    return ca, sb


def _layer_weights(d, l, w, lam, bblk):
    f = {}
    r2 = d.rope // 2
    w_in = w["w_in"][l]
    o = np.cumsum([0, d.q_lora, d.kv_lora, d.rope, d.s5_width, d.gm_width, d.gm_width])
    c_q, c_kv, k_rope, u_s5, u_gm, v_gm = [w_in[:, o[i]:o[i + 1]] for i in range(6)]
    zpad = jnp.zeros((d.d_model, LANES - d.rope), F32)
    k_rot = jnp.concatenate([-k_rope[:, r2:], k_rope[:, :r2]], axis=1)
    f["w_in"] = jnp.concatenate([c_q, c_kv, u_s5, u_gm, v_gm, k_rope, zpad, k_rot, zpad], axis=1).astype(BF16)

    w_uq = w["w_uq"][l].reshape(d.q_lora, d.heads, d.nope + d.rope)
    qn, qr = w_uq[:, :, :d.nope], w_uq[:, :, d.nope:]
    qrot = jnp.concatenate([-qr[:, :, r2:], qr[:, :, :r2]], axis=2)
    hpad = jnp.zeros((d.q_lora, d.heads, LANES - d.rope - d.nope), F32)
    plain = jnp.concatenate([qr, qn, hpad], axis=2).reshape(d.q_lora, d.qk_cols)
    partner = jnp.concatenate([qrot, jnp.zeros_like(qn), hpad], axis=2).reshape(d.q_lora, d.qk_cols)
    f["w_q"] = jnp.concatenate([plain, partner], axis=1).astype(BF16)

    w_uk = w["w_uk"][l]
    kz0 = jnp.zeros((d.kv_lora, d.heads, d.rope), F32)
    kz1 = jnp.zeros((d.kv_lora, d.heads, LANES - d.rope - d.nope), F32)
    f["w_ka"] = jnp.concatenate([kz0, w_uk, kz1], axis=2).reshape(d.kv_lora, d.qk_cols).astype(BF16)
    f["w_ukp"] = jnp.concatenate([kz0, w_uk, kz1], axis=2).transpose(1, 2, 0).astype(BF16)
    w_uv = w["w_uv"][l]
    f["w_uvt"] = w_uv.reshape(d.kv_lora, d.mla_width).T.astype(BF16)
    eye_h = jnp.eye(d.heads, dtype=F32)
    f["w_uv_big"] = (w_uv.transpose(1, 0, 2)[:, :, None, :] * eye_h[:, None, :, None]).reshape(
        d.heads * d.kv_lora, d.mla_width).astype(BF16)

    f["lam"] = lam[l]
    f["bblk"] = bblk[l]
    eye_g = jnp.eye(d.s5_groups, dtype=F32)

    def cblock(cc):
        return (cc.transpose(0, 2, 1)[:, :, None, :] * eye_g[:, None, :, None]).reshape(d.s5_lanes, d.s5_width)

    f["cblk"] = jnp.concatenate([cblock(w["s5_c_re"][l]), -cblock(w["s5_c_im"][l])], axis=0).astype(BF16)
    f["s5_d"] = w["s5_d"][l][None, :]
    f["glu_w"] = w["s5_glu_w"][l].astype(BF16)
    f["glu_b"] = w["s5_glu_b"][l][None, :]

    f["g_gm_v"] = w["g_gm_v"][l][None, :]
    f["gm_ws"] = w["gm_ws"][l]
    gm_bs = w["gm_bs"][l]
    f["gm_bias"] = jnp.repeat(gm_bs.T, d.gm_head_dim, axis=1)
    f["gm_w0"] = jnp.repeat(w["gm_ws"][l][:, 0, 0], d.gm_head_dim)[None, :]
    f["gm_b0"] = jnp.repeat(gm_bs[:, 0], d.gm_head_dim)[None, :]

    for name in ("g_mix", "g_q", "g_kv", "g_ffn", "g_ple"):
        f[name] = w[name][l][None, :]
    for name in ("w_out", "w_up", "w_down", "w_ple", "w_pg"):
        f[name] = w[name][l].astype(BF16)
    return f


def kernel(x_prompt, x_sample, cache_ckv, cache_krope, state_s5_re, state_s5_im, page_table, p_prompt, p_sample, g_mix, w_in, g_q, w_uq, g_kv, w_uk, w_uv, s5_a_re, s5_a_im, s5_log_dt, s5_b_re, s5_b_im, s5_c_re, s5_c_im, s5_d, s5_glu_w, s5_glu_b, g_gm_v, gm_ws, gm_bs, w_out, g_ffn, w_up, w_down, g_ple, w_ple, w_pg, g_final):
    w = dict(g_mix=g_mix, w_in=w_in, g_q=g_q, w_uq=w_uq, g_kv=g_kv, w_uk=w_uk, w_uv=w_uv,
             s5_c_re=s5_c_re, s5_c_im=s5_c_im, s5_d=s5_d, s5_glu_w=s5_glu_w, s5_glu_b=s5_glu_b,
             g_gm_v=g_gm_v, gm_ws=gm_ws, gm_bs=gm_bs, w_out=w_out, g_ffn=g_ffn, w_up=w_up,
             w_down=w_down, g_ple=g_ple, w_ple=w_ple, w_pg=w_pg)
    d = _Dims(w_in, w_uq, w_uk, w_uv, s5_b_re, gm_ws)
    depth = w_in.shape[0]
    batch, seq, dm = x_prompt.shape
    db, dec_seq, _ = x_sample.shape
    assert batch == 1 and dec_seq == 1
    past_len = page_table.shape[1] * cache_ckv.shape[2]

    lam, bblk = _s5_discretise(s5_a_re, s5_a_im, s5_log_dt, s5_b_re, s5_b_im)
    ca_p, sb_p = _rope_tables(d, jnp.arange(seq))
    ca_s, sb_s = _rope_tables(d, jnp.full((db,), past_len))
    g_fin = g_final[None, :]
    cache_krope_t = jnp.swapaxes(cache_krope, 2, 3)

    hp = x_prompt.reshape(seq, dm)
    hs = x_sample.reshape(db, dm)
    outs = [[] for _ in range(9)]
    for l in range(depth):
        lw = _layer_weights(d, l, w, lam, bblk)
        last = l == depth - 1
        q, k, vt, ckv_p, kr_p, os5, ogm, sre_p, sim_p = _proj_prompt(d, hp, ca_p, sb_p, lw)
        att = _flash_attention(d, q, k, vt)
        hp = _out_layer(d, hp, att, os5, ogm, p_prompt[l].reshape(seq, -1), None, lw, g_fin if last else None)
        qs, qlat, ckv_s, kr_s, os5_s, ogm_s, sre_s, sim_s, vn_s = _proj_sample(
            d, hs, ca_s, sb_s, state_s5_re[l].reshape(db, -1), state_s5_im[l].reshape(db, -1), lw)
        olat = _paged_attention(d, l, page_table, qlat, qs, ckv_s, kr_s, cache_ckv, cache_krope_t)
        hs = _out_layer(d, hs, olat.reshape(db, -1), os5_s, ogm_s, p_sample[l].reshape(db, -1),
                        lw["w_uv_big"], lw, g_fin if last else None)
        gshape = (d.s5_groups, d.s5_state)
        for lst, val in zip(outs, (
                ckv_p.reshape(batch, seq, -1), kr_p.reshape(batch, seq, -1),
                ckv_s.reshape(db, dec_seq, -1), kr_s.reshape(db, dec_seq, -1),
                sre_p.reshape(batch, *gshape), sim_p.reshape(batch, *gshape),
                sre_s.reshape(db, *gshape), sim_s.reshape(db, *gshape),
                vn_s.reshape(db, dec_seq, -1))):
            lst.append(val)
    return (hp.reshape(batch, seq, dm), hs.reshape(db, dec_seq, dm), *[jnp.stack(o) for o in outs])
```
